```python
import math
import jax
import jax.numpy as jnp
from jax import lax
import numpy as np

D_MODEL = 2048
BATCH = 2
SEQ = 8192
DEPTH = 2

N_META = 16
BLOCK = 128
PAD_FRONT = BLOCK - N_META
GROUP_W = D_MODEL // 4
DIFF_HEADS = 4
DIFF_QK = 64
DIFF_V = GROUP_W // DIFF_HEADS
SB_HEADS = 8
SB_DIM = GROUP_W // SB_HEADS
SSM_CH = GROUP_W
SSM_GROUP = 16
SSM_NG = SSM_CH // SSM_GROUP
SSM_STATE = 64
RET_HEADS = 4
RET_DIM = GROUP_W // RET_HEADS
FFN_HIDDEN = -(-8 * D_MODEL // (3 * 256)) * 256
ROPE_THETA = 500000.0
ROPE_DIM = DIFF_QK // 4
RET_THETA = 10000.0
NEG_INF = -1e30
EPS = 1e-6
IN_SPLITS = (DIFF_HEADS * 2 * DIFF_QK, DIFF_HEADS * 2 * DIFF_QK, DIFF_HEADS * DIFF_V,
             SB_HEADS * SB_DIM, SB_HEADS * SB_DIM, SB_HEADS * SB_DIM,
             SSM_CH,
             RET_HEADS * RET_DIM, RET_HEADS * RET_DIM, RET_HEADS * RET_DIM, RET_HEADS * RET_DIM)
IN_COLS = sum(IN_SPLITS)

kernel_name = 'hybrid_diff_stickbreak_s5_retention_block'


def rms_norm(x, g):
    xf = x.astype(jnp.float32)
    y = xf * lax.rsqrt(jnp.mean(xf * xf, axis=-1, keepdims=True) + EPS)
    return (y * g.astype(jnp.float32)).astype(x.dtype)


def head_layer_norm(x, g):
    xf = x.astype(jnp.float32)
    mu = jnp.mean(xf, axis=-1, keepdims=True)
    xc = xf - mu
    y = xc * lax.rsqrt(jnp.mean(xc * xc, axis=-1, keepdims=True) + EPS)
    return (y * g.astype(jnp.float32)).astype(x.dtype)


def rotary(x, pos, rot_dim, theta):
    half = rot_dim // 2
    inv_freq = 1.0 / (theta ** (jnp.arange(half, dtype=jnp.float32) * (2.0 / rot_dim)))
    ang = pos.astype(jnp.float32)[:, None] * inv_freq[None, :]
    cos = jnp.cos(ang)[None, :, None, :]
    sin = jnp.sin(ang)[None, :, None, :]
    xr = x[..., :rot_dim].astype(jnp.float32)
    x1, x2 = xr[..., :half], xr[..., half:]
    rot = jnp.concatenate([x1 * cos - x2 * sin, x2 * cos + x1 * sin], axis=-1).astype(x.dtype)
    return jnp.concatenate([rot, x[..., rot_dim:]], axis=-1)


def pad_front(t):
    pad = [(0, 0)] * t.ndim
    pad[1] = (PAD_FRONT, 0)
    return jnp.pad(t, pad)


def to_blocks(t):
    b, lp = t.shape[0], t.shape[1]
    t = t.reshape((b, lp // BLOCK, BLOCK) + t.shape[2:])
    return jnp.moveaxis(t, 1, 0)


def from_blocks(t):
    t = jnp.moveaxis(t, 0, 1)
    return t.reshape((t.shape[0], t.shape[1] * t.shape[2]) + t.shape[3:])


def diff_attention(q, k, v, lam):
    lp = q.shape[1]
    kpos = jnp.arange(lp)
    key_ok = kpos >= PAD_FRONT
    kf = k.astype(jnp.float32)
    vf = v.astype(jnp.float32)
    scale = DIFF_QK ** -0.5

    def block(args):
        qb, i = args
        qpos = i * BLOCK + jnp.arange(BLOCK)
        mask = (kpos[None, :] <= qpos[:, None]) & key_ok[None, :]
        s = jnp.einsum('bqhcd,bkhcd->bhcqk', qb.astype(jnp.float32), kf) * scale
        s = jnp.where(mask, s, NEG_INF)
        p = jax.nn.softmax(s, axis=-1)
        w = p[:, :, 0] - lam * p[:, :, 1]
        return jnp.einsum('bhqk,bkhd->bqhd', w, vf).astype(v.dtype)

    out = lax.map(block, (to_blocks(q), jnp.arange(lp // BLOCK)))
    return from_blocks(out)


def stick_breaking_attention(q, k, v):
    lp = q.shape[1]
    kpos = jnp.arange(lp)
    key_ok = kpos >= PAD_FRONT
    kf = k.astype(jnp.float32)
    vf = v.astype(jnp.float32)
    scale = SB_DIM ** -0.5

    def block(args):
        qb, i = args
        qpos = i * BLOCK + jnp.arange(BLOCK)
        mask = (kpos[None, :] < qpos[:, None]) & key_ok[None, :]
        z = jnp.einsum('bqhd,bkhd->bhqk', qb.astype(jnp.float32), kf) * scale
        log_not_break = jnp.where(mask, jax.nn.log_sigmoid(-z), 0.0)
        tail = lax.cumsum(log_not_break, axis=3, reverse=True)
        log_w = jnp.where(mask, z + tail, NEG_INF)
        w = jnp.exp(log_w)
        return jnp.einsum('bhqk,bkhd->bqhd', w, vf).astype(v.dtype)

    out = lax.map(block, (to_blocks(q), jnp.arange(lp // BLOCK)))
    return from_blocks(out)


def complex_linear_combine(e1, e2):
    a1r, a1i, b1r, b1i = e1
    a2r, a2i, b2r, b2i = e2
    ar = a2r * a1r - a2i * a1i
    ai = a2r * a1i + a2i * a1r
    br = a2r * b1r - a2i * b1i + b2r
    bi = a2r * b1i + a2i * b1r + b2i
    return (ar, ai, br, bi)


def s5_mixer(u, a_re, a_im, log_dt, b_re, b_im, c_re, c_im, d, w_glu, b_glu):
    f32 = jnp.float32
    bsz, seq_len = u.shape[0], u.shape[1]
    uf = u.astype(f32).reshape(bsz, seq_len, SSM_NG, SSM_GROUP)
    dt = jnp.exp(log_dt.astype(f32))[:, None]
    ar, ai = a_re.astype(f32), a_im.astype(f32)
    mag = jnp.exp(ar * dt)
    abar_re = mag * jnp.cos(ai * dt)
    abar_im = mag * jnp.sin(ai * dt)
    nr, ni = abar_re - 1.0, abar_im
    den = ar * ar + ai * ai
    fr = (nr * ar + ni * ai) / den
    fi = (ni * ar - nr * ai) / den
    br, bi = b_re.astype(f32), b_im.astype(f32)
    bbar_re = fr[..., None] * br - fi[..., None] * bi
    bbar_im = fr[..., None] * bi + fi[..., None] * br
    bu_re = jnp.einsum('gph,blgh->lbgp', bbar_re, uf)
    bu_im = jnp.einsum('gph,blgh->lbgp', bbar_im, uf)
    a_seq_re = jnp.broadcast_to(abar_re[None, None], (seq_len, 1, SSM_NG, SSM_STATE))
    a_seq_im = jnp.broadcast_to(abar_im[None, None], (seq_len, 1, SSM_NG, SSM_STATE))
    _, _, xr, xi = lax.associative_scan(complex_linear_combine, (a_seq_re, a_seq_im, bu_re, bu_im), axis=0)
    y = (jnp.einsum('ghp,lbgp->blgh', c_re.astype(f32), xr)
         - jnp.einsum('ghp,lbgp->blgh', c_im.astype(f32), xi))
    y = y.reshape(bsz, seq_len, SSM_CH) + d.astype(f32) * u.astype(f32)
    g = jax.nn.gelu(y)
    out = g * jax.nn.sigmoid(g @ w_glu.astype(f32) + b_glu.astype(f32))
    return out.astype(u.dtype)


def retention(q, k, v):
    f32 = jnp.float32
    bsz = q.shape[0]
    qf = q.astype(f32)
    kf = k.astype(f32) * (RET_DIM ** -0.5)
    vf = v.astype(f32)
    log_gamma = jnp.log1p(-jnp.exp2(-5.0 - jnp.arange(RET_HEADS, dtype=f32)))
    idx = jnp.arange(BLOCK, dtype=f32)
    rel = idx[:, None] - idx[None, :]
    dmat = jnp.where(rel >= 0, jnp.exp(log_gamma[:, None, None] * jnp.maximum(rel, 0.0)), 0.0)
    q_decay = jnp.exp(log_gamma[:, None] * (idx + 1.0)).T[None, :, :, None]
    k_decay = jnp.exp(log_gamma[:, None] * (BLOCK - 1.0 - idx))
    chunk_decay = jnp.exp(log_gamma * BLOCK)[None, :, None, None]

    def step(state, inp):
        qc, kc, vc = inp
        s = jnp.einsum('bqhd,bkhd->bhqk', qc, kc) * dmat
        o_inner = jnp.einsum('bhqk,bkhe->bqhe', s, vc)
        o_cross = jnp.einsum('bqhd,bhde->bqhe', qc, state) * q_decay
        state = chunk_decay * state + jnp.einsum('bkhd,bkhe,hk->bhde', kc, vc, k_decay)
        return state, o_inner + o_cross

    state0 = jnp.zeros((bsz, RET_HEADS, RET_DIM, RET_DIM), f32)
    _, out = lax.scan(step, state0, (to_blocks(qf), to_blocks(kf), to_blocks(vf)))
    return from_blocks(out).astype(q.dtype)


def hybrid_layer(x, pos, layer_idx, norm_mix_pre, w_in, lq1, lk1, lq2, lk2, diff_norm, sb_norm,
                 a_re, a_im, log_dt, b_re, b_im, c_re, c_im, ssm_d, w_glu, b_glu, ssm_norm,
                 ret_norm, w_out, norm_mix_post, norm_ffn_pre, w_gate, w_up, w_down, norm_ffn_post):
    f32 = jnp.float32
    bsz, seq_len, _ = x.shape
    h = rms_norm(x, norm_mix_pre)
    proj = h @ w_in
    split_points = [int(s) for s in np.cumsum(IN_SPLITS)[:-1]]
    dq, dk, dv, sq, sk, sv, su, rq, rk, rv, rg = jnp.split(proj, split_points, axis=-1)

    lam_init = 0.8 - 0.6 * math.exp(-0.3 * layer_idx)
    lam = (jnp.exp(jnp.sum(lq1.astype(f32) * lk1.astype(f32)))
           - jnp.exp(jnp.sum(lq2.astype(f32) * lk2.astype(f32))) + lam_init)
    dq = rotary(dq.reshape(bsz, seq_len, DIFF_HEADS * 2, DIFF_QK), pos, ROPE_DIM, ROPE_THETA)
    dk = rotary(dk.reshape(bsz, seq_len, DIFF_HEADS * 2, DIFF_QK), pos, ROPE_DIM, ROPE_THETA)
    dq = dq.reshape(bsz, seq_len, DIFF_HEADS, 2, DIFF_QK)
    dk = dk.reshape(bsz, seq_len, DIFF_HEADS, 2, DIFF_QK)
    dv = dv.reshape(bsz, seq_len, DIFF_HEADS, DIFF_V)
    o_diff = diff_attention(pad_front(dq), pad_front(dk), pad_front(dv), lam)[:, PAD_FRONT:]
    o_diff = (rms_norm(o_diff, diff_norm) * (1.0 - lam_init)).reshape(bsz, seq_len, GROUP_W)

    sq = sq.reshape(bsz, seq_len, SB_HEADS, SB_DIM)
    sk = sk.reshape(bsz, seq_len, SB_HEADS, SB_DIM)
    sv = sv.reshape(bsz, seq_len, SB_HEADS, SB_DIM)
    o_sb = stick_breaking_attention(pad_front(sq), pad_front(sk), pad_front(sv))[:, PAD_FRONT:]
    o_sb = rms_norm(o_sb, sb_norm).reshape(bsz, seq_len, GROUP_W)

    o_ssm = rms_norm(s5_mixer(su, a_re, a_im, log_dt, b_re, b_im, c_re, c_im, ssm_d, w_glu, b_glu), ssm_norm)

    rq = rotary(rq.reshape(bsz, seq_len, RET_HEADS, RET_DIM), pos, RET_DIM, RET_THETA)
    rk = rotary(rk.reshape(bsz, seq_len, RET_HEADS, RET_DIM), pos, RET_DIM, RET_THETA)
    rv = rv.reshape(bsz, seq_len, RET_HEADS, RET_DIM)
    o_ret = retention(pad_front(rq), pad_front(rk), pad_front(rv))[:, PAD_FRONT:]
    o_ret = head_layer_norm(o_ret, ret_norm).reshape(bsz, seq_len, GROUP_W) * jax.nn.silu(rg)

    mix = jnp.concatenate([o_diff, o_sb, o_ssm, o_ret], axis=-1)
    x = x + rms_norm(mix @ w_out, norm_mix_post)

    hf = rms_norm(x, norm_ffn_pre)
    ffn = (jax.nn.silu(hf @ w_gate) * (hf @ w_up)) @ w_down
    return x + rms_norm(ffn, norm_ffn_post)


def setup_inputs(seed: int = 0) -> dict:
    key = jax.random.key(seed)
    ks = jax.random.split(key, 32)
    f32 = jnp.float32

    def nrm(k, shape, scale):
        return jax.random.normal(k, shape, f32) * scale

    def gain(k, shape):
        return 1.0 + 0.02 * jax.random.normal(k, shape, f32)

    n_idx = jnp.arange(SSM_STATE, dtype=f32)
    return {
        'x': nrm(ks[0], (BATCH, SEQ, D_MODEL), 1.0),
        'meta_tokens': nrm(ks[1], (N_META, D_MODEL), 1.0),
        'norm_mix_pre': gain(ks[2], (DEPTH, D_MODEL)),
        'w_in': nrm(ks[3], (DEPTH, D_MODEL, IN_COLS), D_MODEL ** -0.5),
        'diff_lambda_q1': nrm(ks[4], (DEPTH, DIFF_QK), 0.1),
        'diff_lambda_k1': nrm(ks[5], (DEPTH, DIFF_QK), 0.1),
        'diff_lambda_q2': nrm(ks[6], (DEPTH, DIFF_QK), 0.1),
        'diff_lambda_k2': nrm(ks[7], (DEPTH, DIFF_QK), 0.1),
        'diff_norm': gain(ks[8], (DEPTH, DIFF_V)),
        'sb_norm': gain(ks[9], (DEPTH, SB_DIM)),
        'ssm_a_re': -0.5 + nrm(ks[10], (DEPTH, SSM_NG, SSM_STATE), 0.01),
        'ssm_a_im': math.pi * n_idx + nrm(ks[11], (DEPTH, SSM_NG, SSM_STATE), 0.01),
        'ssm_log_dt': jax.random.uniform(ks[12], (DEPTH, SSM_NG), f32, math.log(1e-3), math.log(1e-1)),
        'ssm_b_re': nrm(ks[13], (DEPTH, SSM_NG, SSM_STATE, SSM_GROUP), (2 * SSM_GROUP) ** -0.5),
        'ssm_b_im': nrm(ks[14], (DEPTH, SSM_NG, SSM_STATE, SSM_GROUP), (2 * SSM_GROUP) ** -0.5),
        'ssm_c_re': nrm(ks[15], (DEPTH, SSM_NG, SSM_GROUP, SSM_STATE), (2 * SSM_STATE) ** -0.5),
        'ssm_c_im': nrm(ks[16], (DEPTH, SSM_NG, SSM_GROUP, SSM_STATE), (2 * SSM_STATE) ** -0.5),
        'ssm_d': nrm(ks[17], (DEPTH, SSM_CH), 1.0),
        'ssm_w_glu': nrm(ks[18], (DEPTH, SSM_CH, SSM_CH), SSM_CH ** -0.5),
        'ssm_b_glu': nrm(ks[19], (DEPTH, SSM_CH), 0.01),
        'ssm_norm': gain(ks[20], (DEPTH, SSM_CH)),
        'ret_norm': gain(ks[21], (DEPTH, RET_DIM)),
        'w_out': nrm(ks[22], (DEPTH, D_MODEL, D_MODEL), D_MODEL ** -0.5),
        'norm_mix_post': gain(ks[23], (DEPTH, D_MODEL)),
        'norm_ffn_pre': gain(ks[24], (DEPTH, D_MODEL)),
        'w_ffn_gate': nrm(ks[25], (DEPTH, D_MODEL, FFN_HIDDEN), D_MODEL ** -0.5),
        'w_ffn_up': nrm(ks[26], (DEPTH, D_MODEL, FFN_HIDDEN), D_MODEL ** -0.5),
        'w_ffn_down': nrm(ks[27], (DEPTH, FFN_HIDDEN, D_MODEL), FFN_HIDDEN ** -0.5),
        'norm_ffn_post': gain(ks[28], (DEPTH, D_MODEL)),
    }


def reference(x, meta_tokens, norm_mix_pre, w_in, diff_lambda_q1, diff_lambda_k1, diff_lambda_q2,
              diff_lambda_k2, diff_norm, sb_norm, ssm_a_re, ssm_a_im, ssm_log_dt, ssm_b_re, ssm_b_im,
              ssm_c_re, ssm_c_im, ssm_d, ssm_w_glu, ssm_b_glu, ssm_norm, ret_norm, w_out,
              norm_mix_post, norm_ffn_pre, w_ffn_gate, w_ffn_up, w_ffn_down, norm_ffn_post):
    bsz = x.shape[0]
    meta = jnp.broadcast_to(meta_tokens[None].astype(x.dtype), (bsz, N_META, D_MODEL))
    h = jnp.concatenate([meta, x], axis=1)
    pos = jnp.arange(h.shape[1], dtype=jnp.int32)
    for l in range(DEPTH):
        h = hybrid_layer(h, pos, l, norm_mix_pre[l], w_in[l], diff_lambda_q1[l], diff_lambda_k1[l],
                         diff_lambda_q2[l], diff_lambda_k2[l], diff_norm[l], sb_norm[l],
                         ssm_a_re[l], ssm_a_im[l], ssm_log_dt[l], ssm_b_re[l], ssm_b_im[l],
                         ssm_c_re[l], ssm_c_im[l], ssm_d[l], ssm_w_glu[l], ssm_b_glu[l], ssm_norm[l],
                         ret_norm[l], w_out[l], norm_mix_post[l], norm_ffn_pre[l],
                         w_ffn_gate[l], w_ffn_up[l], w_ffn_down[l], norm_ffn_post[l])
    return h[:, N_META:]
```

```python
import functools
import math

import jax
import jax.numpy as jnp
from jax import lax
from jax.experimental import pallas as pl
from jax.experimental.pallas import tpu as pltpu

F32 = jnp.float32
BF16 = jnp.bfloat16

D_MODEL = 2048
N_META = 16
GROUP_W = D_MODEL // 4
DIFF_HEADS = 4
DIFF_QK = 64
DIFF_V = GROUP_W // DIFF_HEADS
SB_HEADS = 8
SB_DIM = GROUP_W // SB_HEADS
SSM_GROUP = 16
SSM_NG = GROUP_W // SSM_GROUP
SSM_STATE = 64
RET_HEADS = 4
RET_DIM = GROUP_W // RET_HEADS
FFN_HIDDEN = -(-8 * D_MODEL // (3 * 256)) * 256
IN_COLS = 11 * GROUP_W
ROPE_THETA = 500000.0
ROPE_DIM = DIFF_QK // 4
RET_THETA = 10000.0
NEG_INF = -1e30
EPS = 1e-6

LANES = 128
MXU_DIM = 256
SEQ_TILE = 768
SSM_CHUNK = 16
VMEM_LIMIT = 52 * 1024 * 1024

COL_DQ, COL_DK, COL_DV, COL_SQ, COL_SK, COL_SV, COL_SU, COL_RQ, COL_RK, COL_RV, COL_RG = range(11)


def _padded_len(n_tok):
    lp = -(-n_tok // MXU_DIM) * MXU_DIM
    while lp % SEQ_TILE:
        lp += MXU_DIM
    return lp


def _params(*sem):
    return pltpu.CompilerParams(dimension_semantics=sem, vmem_limit_bytes=VMEM_LIMIT)


def _in_proj_kernel(x_ref, g_ref, w_ref, dc_ref, dsa_ref, dsb_ref, rc_ref, rs_ref, o_ref, h_scr,
                    *, tm, tiles_per_batch, pad_front):
    i = pl.program_id(0)
    j = pl.program_id(1)

    @pl.when(j == 0)
    def _():
        x = x_ref[...]
        y = x * lax.rsqrt(jnp.mean(x * x, axis=-1, keepdims=True) + EPS) * g_ref[...]
        pos = (i % tiles_per_batch) * tm + lax.broadcasted_iota(jnp.int32, (tm, 1), 0)
        h_scr[...] = jnp.where(pos >= pad_front, y, 0.0).astype(BF16)

    acc = jnp.dot(h_scr[...], w_ref[...], preferred_element_type=F32)

    is_diff = (j == COL_DQ) | (j == COL_DK)
    is_ret = (j == COL_RQ) | (j == COL_RK)

    @pl.when(is_diff)
    def _():
        scale = jnp.where(j == COL_DQ, DIFF_QK ** -0.5, 1.0).astype(F32)
        c, sa, sb = dc_ref[...], dsa_ref[...], dsb_ref[...]
        for n in range(GROUP_W // LANES):
            a = acc[:, n * LANES:(n + 1) * LANES]
            r = a * c + pltpu.roll(a, LANES - ROPE_DIM // 2, 1) * sa + pltpu.roll(a, ROPE_DIM // 2, 1) * sb
            o_ref[:, n * LANES:(n + 1) * LANES] = (r * scale).astype(BF16)

    @pl.when(is_ret)
    def _():
        scale = jnp.where(j == COL_RK, RET_DIM ** -0.5, 1.0).astype(F32)
        c, s = rc_ref[...], rs_ref[...]
        for n in range(GROUP_W // LANES):
            a = acc[:, n * LANES:(n + 1) * LANES]
            r = a * c + pltpu.roll(a, RET_DIM // 2, 1) * s
            o_ref[:, n * LANES:(n + 1) * LANES] = (r * scale).astype(BF16)

    @pl.when(jnp.logical_not(is_diff | is_ret))
    def _():
        scale = jnp.where(j == COL_SQ, SB_DIM ** -0.5, 1.0).astype(F32)
        o_ref[...] = (acc * scale).astype(BF16)


def _in_proj(x2d, gain, w_bf16, tabs, *, lp, pad_front):
    rows = x2d.shape[0]
    tm = SEQ_TILE
    tiles_per_batch = lp // tm
    n_col = IN_COLS // GROUP_W
    tab_spec = pl.BlockSpec((tm, LANES), lambda i, j: (i % tiles_per_batch, 0))
    return pl.pallas_call(
        functools.partial(_in_proj_kernel, tm=tm, tiles_per_batch=tiles_per_batch, pad_front=pad_front),
        grid=(rows // tm, n_col),
        in_specs=[
            pl.BlockSpec((tm, D_MODEL), lambda i, j: (i, 0)),
            pl.BlockSpec((1, D_MODEL), lambda i, j: (0, 0)),
            pl.BlockSpec((D_MODEL, GROUP_W), lambda i, j: (0, j)),
            tab_spec, tab_spec, tab_spec, tab_spec, tab_spec,
        ],
        out_specs=pl.BlockSpec((tm, GROUP_W), lambda i, j: (i, j)),
        out_shape=jax.ShapeDtypeStruct((rows, IN_COLS), BF16),
        scratch_shapes=[pltpu.VMEM((tm, D_MODEL), BF16)],
        compiler_params=_params("parallel", "arbitrary"),
        name="in_proj",
    )(x2d, gain.reshape(1, D_MODEL), w_bf16, *tabs)


def _rotary_tables(lp, pad_front):
    pos = (jnp.arange(lp, dtype=jnp.int32) - pad_front).astype(F32)[:, None]
    lane = jnp.arange(LANES)
    half = ROPE_DIM // 2
    inv = 1.0 / (ROPE_THETA ** (jnp.arange(half, dtype=F32) * (2.0 / ROPE_DIM)))
    ang = pos * inv[None, :]
    d = lane % DIFF_QK
    cos_l = jnp.cos(ang)[:, d % half]
    sin_l = jnp.sin(ang)[:, d % half]
    dc = jnp.where(d < ROPE_DIM, cos_l, 1.0)
    dsa = jnp.where(d < half, -sin_l, 0.0)
    dsb = jnp.where((d >= half) & (d < ROPE_DIM), sin_l, 0.0)
    rhalf = RET_DIM // 2
    rinv = 1.0 / (RET_THETA ** (jnp.arange(rhalf, dtype=F32) * (2.0 / RET_DIM)))
    rang = pos * rinv[None, :]
    r = lane % RET_DIM
    rc = jnp.cos(rang)[:, r % rhalf]
    rs = jnp.where(r < rhalf, -1.0, 1.0) * jnp.sin(rang)[:, r % rhalf]
    return dc, dsa, dsb, rc, rs


def _diff_attn_kernel(lam_ref, q_ref, k_ref, v_ref, gn_ref, o_ref, m_scr, l_scr, acc_scr,
                      *, t, pad_front, lam_init):
    i = pl.program_id(2)
    j = pl.program_id(3)

    @pl.when(j == 0)
    def _():
        m_scr[...] = jnp.full(m_scr.shape, NEG_INF, F32)
        l_scr[...] = jnp.zeros(l_scr.shape, F32)
        acc_scr[...] = jnp.zeros(acc_scr.shape, F32)

    def step(masked):
        q, k, v = q_ref[...], k_ref[...], v_ref[...]
        if masked:
            qpos = i * t + lax.broadcasted_iota(jnp.int32, (t, t), 0)
            kpos = j * t + lax.broadcasted_iota(jnp.int32, (t, t), 1)
            mask = (kpos <= qpos) & (kpos >= pad_front)
        lane = lax.broadcasted_iota(jnp.int32, q.shape, 1)
        for c in range(2):
            qc = jnp.where((lane >= c * DIFF_QK) & (lane < (c + 1) * DIFF_QK), q, jnp.zeros_like(q))
            s = lax.dot_general(qc, k, (((1,), (1,)), ((), ())), preferred_element_type=F32)
            if masked:
                s = jnp.where(mask, s, NEG_INF)
            m_prev = m_scr[c]
            m_new = jnp.maximum(m_prev, jnp.max(s, axis=-1, keepdims=True))
            alpha = jnp.exp(m_prev - m_new)
            p = jnp.exp(s - m_new)
            l_scr[c] = alpha * l_scr[c] + jnp.sum(p, axis=-1, keepdims=True)
            acc_scr[c] = alpha * acc_scr[c] + jnp.dot(p.astype(BF16), v, preferred_element_type=F32)
            m_scr[c] = m_new

    needs_mask = (j == i) | (j == 0)

    @pl.when(needs_mask & (j <= i))
    def _():
        step(True)

    @pl.when(jnp.logical_not(needs_mask) & (j < i))
    def _():
        step(False)

    @pl.when(j == i)
    def _():
        lv = lam_ref[...]
        lam = (jnp.exp(jnp.sum(lv[0:1] * lv[1:2], axis=-1, keepdims=True))
               - jnp.exp(jnp.sum(lv[2:3] * lv[3:4], axis=-1, keepdims=True)) + lam_init)
        o = acc_scr[0] / l_scr[0] - lam * (acc_scr[1] / l_scr[1])
        y = o * lax.rsqrt(jnp.mean(o * o, axis=-1, keepdims=True) + EPS) * gn_ref[...]
        o_ref[...] = (y * (1.0 - lam_init)).astype(BF16)


def _diff_attention(proj, lam_vecs, gn, *, bsz, lp, pad_front, lam_init):
    t = SEQ_TILE
    nt = lp // t
    kv_idx = lambda off: (lambda b, h, i, j: (b * nt + jnp.minimum(j, i), off + h))
    return pl.pallas_call(
        functools.partial(_diff_attn_kernel, t=t, pad_front=pad_front, lam_init=lam_init),
        grid=(bsz, DIFF_HEADS, nt, nt),
        in_specs=[
            pl.BlockSpec((4, DIFF_QK), lambda b, h, i, j: (0, 0)),
            pl.BlockSpec((t, LANES), lambda b, h, i, j: (b * nt + i, COL_DQ * 4 + h)),
            pl.BlockSpec((t, LANES), kv_idx(COL_DK * 4)),
            pl.BlockSpec((t, LANES), kv_idx(COL_DV * 4)),
            pl.BlockSpec((1, DIFF_V), lambda b, h, i, j: (0, 0)),
        ],
        out_specs=pl.BlockSpec((t, LANES), lambda b, h, i, j: (b * nt + i, h)),
        out_shape=jax.ShapeDtypeStruct((bsz * lp, GROUP_W), BF16),
        scratch_shapes=[pltpu.VMEM((2, t, 1), F32), pltpu.VMEM((2, t, 1), F32), pltpu.VMEM((2, t, DIFF_V), F32)],
        compiler_params=_params("parallel", "parallel", "parallel", "arbitrary"),
        name="diff_attn",
    )(lam_vecs, proj, proj, proj, gn.reshape(1, DIFF_V))


def _sb_attn_kernel(q_ref, k_ref, v_ref, tri_ref, gn_ref, o_ref, r_scr, acc_scr, *, t, sub, pad_front):
    i = pl.program_id(2)
    jj = pl.program_id(3)
    j = i - jj

    @pl.when(jj == 0)
    def _():
        r_scr[...] = jnp.zeros(r_scr.shape, F32)
        acc_scr[...] = jnp.zeros(acc_scr.shape, F32)

    def step(masked):
        q, k, v = q_ref[...], k_ref[...], v_ref[...]
        tri = tri_ref[...]
        lane = lax.broadcasted_iota(jnp.int32, q.shape, 1)
        for hh in range(LANES // SB_DIM):
            qh = jnp.where((lane >= hh * SB_DIM) & (lane < (hh + 1) * SB_DIM), q, jnp.zeros_like(q))
            run = r_scr[hh]
            acc = acc_scr[hh]
            for sc in reversed(range(t // sub)):
                rows = slice(sc * sub, (sc + 1) * sub)
                z = lax.dot_general(qh, k[rows], (((1,), (1,)), ((), ())), preferred_element_type=F32)
                lnb = jnp.minimum(-z, 0.0) - jnp.log1p(jnp.exp(-jnp.abs(z)))
                if masked:
                    qpos = i * t + lax.broadcasted_iota(jnp.int32, (t, sub), 0)
                    kpos = j * t + sc * sub + lax.broadcasted_iota(jnp.int32, (t, sub), 1)
                    mask = (kpos < qpos) & (kpos >= pad_front)
                    lnb = jnp.where(mask, lnb, 0.0)
                hi = lnb.astype(BF16)
                lo = (lnb - hi.astype(F32)).astype(BF16)
                tail = (jnp.dot(hi, tri, preferred_element_type=F32)
                        + jnp.dot(lo, tri, preferred_element_type=F32))
                log_w = z + tail + run
                if masked:
                    log_w = jnp.where(mask, log_w, NEG_INF)
                w = jnp.exp(log_w)
                acc = acc + jnp.dot(w.astype(BF16), v[rows], preferred_element_type=F32)
                run = run + jnp.sum(lnb, axis=-1, keepdims=True)
            r_scr[hh] = run
            acc_scr[hh] = acc

    needs_mask = (jj == 0) | (j == 0)

    @pl.when(needs_mask)
    def _():
        step(True)

    @pl.when(jnp.logical_not(needs_mask))
    def _():
        step(False)

    @pl.when(j == 0)
    def _():
        lane = lax.broadcasted_iota(jnp.int32, (t, LANES), 1)
        y = jnp.zeros((t, LANES), F32)
        for hh in range(LANES // SB_DIM):
            in_head = (lane >= hh * SB_DIM) & (lane < (hh + 1) * SB_DIM)
            o = jnp.where(in_head, acc_scr[hh], 0.0)
            ms = jnp.sum(o * o, axis=-1, keepdims=True) * (1.0 / SB_DIM)
            y = y + o * lax.rsqrt(ms + EPS)
        o_ref[...] = (y * gn_ref[...]).astype(BF16)


def _sb_attention(proj, tri, gn, *, bsz, lp, pad_front):
    t = SEQ_TILE
    nt = lp // t
    n_pair = GROUP_W // LANES
    kv_idx = lambda off: (lambda b, h, i, jj: (b * nt + jnp.maximum(i - jj, 0), off + h))
    return pl.pallas_call(
        functools.partial(_sb_attn_kernel_guarded, t=t, sub=MXU_DIM, pad_front=pad_front),
        grid=(bsz, n_pair, nt, nt),
        in_specs=[
            pl.BlockSpec((t, LANES), lambda b, h, i, jj: (b * nt + i, COL_SQ * 4 + h)),
            pl.BlockSpec((t, LANES), kv_idx(COL_SK * 4)),
            pl.BlockSpec((t, LANES), kv_idx(COL_SV * 4)),
            pl.BlockSpec((MXU_DIM, MXU_DIM), lambda b, h, i, jj: (0, 0)),
            pl.BlockSpec((1, LANES), lambda b, h, i, jj: (0, 0)),
        ],
        out_specs=pl.BlockSpec((t, LANES), lambda b, h, i, jj: (b * nt + i, h)),
        out_shape=jax.ShapeDtypeStruct((bsz * lp, GROUP_W), BF16),
        scratch_shapes=[pltpu.VMEM((2, t, 1), F32), pltpu.VMEM((2, t, LANES), F32)],
        compiler_params=_params("parallel", "parallel", "parallel", "arbitrary"),
        name="sb_attn",
    )(proj, proj, proj, tri, jnp.tile(gn.reshape(1, SB_DIM), (1, LANES // SB_DIM)))


def _sb_attn_kernel_guarded(*refs, t, sub, pad_front):
    @pl.when(pl.program_id(3) <= pl.program_id(2))
    def _():
        _sb_attn_kernel(*refs, t=t, sub=sub, pad_front=pad_front)


def _s5_intra_kernel(u_ref, m_ref, bm_ref, y_ref, s_ref):
    u = u_ref[...]
    y_ref[...] = jnp.dot(u, m_ref[0], preferred_element_type=F32)
    s_ref[...] = jnp.dot(u, bm_ref[0], preferred_element_type=F32)


def _s5_scan_kernel(s_ref, a1_ref, a2_ref, x_ref, *, n_chunk, bsz, groups):
    a1 = a1_ref[...]
    a2 = a2_ref[...]
    w = groups * LANES

    def body(c, carry):
        new = []
        for b in range(bsz):
            x, xs = carry[b]
            row = s_ref[pl.ds(b * n_chunk + c, 1), :]
            x_ref[pl.ds(b * n_chunk + c, 1), :] = x
            s_x = jnp.concatenate([row[:, g * 2 * LANES:g * 2 * LANES + LANES] for g in range(groups)], axis=1)
            s_xs = jnp.concatenate([row[:, g * 2 * LANES + LANES:(g + 1) * 2 * LANES] for g in range(groups)], axis=1)
            new.append((a1 * x + a2 * xs + s_x, a1 * xs - a2 * x + s_xs))
        return tuple(new)

    zero = jnp.zeros((1, w), F32)
    lax.fori_loop(0, n_chunk, body, tuple((zero, zero) for _ in range(bsz)))


def _s5_cross_kernel(yi_ref, x_ref, cm_ref, y_ref):
    y_ref[...] = yi_ref[...] + jnp.dot(x_ref[...].astype(BF16), cm_ref[0], preferred_element_type=F32)


def _s5_out_kernel(y_ref, u_ref, d_ref, w_ref, b_ref, gn_ref, o_ref):
    y = y_ref[...] + d_ref[...] * u_ref[...].astype(F32)
    g = 0.5 * y * (1.0 + jnp.tanh(math.sqrt(2.0 / math.pi) * (y + 0.044715 * (y * y * y))))
    gate = jnp.dot(g.astype(BF16), w_ref[...], preferred_element_type=F32) + b_ref[...]
    o = g * (1.0 / (1.0 + jnp.exp(-gate)))
    o_ref[...] = (o * lax.rsqrt(jnp.mean(o * o, axis=-1, keepdims=True) + EPS) * gn_ref[...]).astype(BF16)


def _s5_matrices(a_re, a_im, log_dt, b_re, b_im, c_re, c_im):
    hp = lax.Precision.HIGHEST
    t = SSM_CHUNK
    dt = jnp.exp(log_dt.astype(F32))[:, None]
    ar, ai = a_re.astype(F32), a_im.astype(F32)
    dpow = jnp.arange(t + 1, dtype=F32)[:, None, None]
    mag = jnp.exp(dpow * (ar * dt)[None])
    pr = mag * jnp.cos(dpow * (ai * dt)[None])
    pi = mag * jnp.sin(dpow * (ai * dt)[None])
    nr, ni = pr[1] - 1.0, pi[1]
    den = ar * ar + ai * ai
    fr = (nr * ar + ni * ai) / den
    fi = (ni * ar - nr * ai) / den
    br, bi = b_re.astype(F32), b_im.astype(F32)
    bbr = fr[..., None] * br - fi[..., None] * bi
    bbi = fr[..., None] * bi + fi[..., None] * br
    cr, ci = c_re.astype(F32), c_im.astype(F32)
    car = cr[None] * pr[:, :, None, :] - ci[None] * pi[:, :, None, :]
    cai = cr[None] * pi[:, :, None, :] + ci[None] * pr[:, :, None, :]
    kern = (jnp.einsum('dghp,gpk->dghk', car[:t], bbr, precision=hp)
            - jnp.einsum('dghp,gpk->dghk', cai[:t], bbi, precision=hp))
    lag = jnp.arange(t)[None, :] - jnp.arange(t)[:, None]
    m = jnp.where((lag >= 0)[:, :, None, None, None], kern[jnp.maximum(lag, 0)], 0.0)
    m = jnp.transpose(m, (2, 0, 4, 1, 3)).reshape(SSM_NG, t * SSM_GROUP, t * SSM_GROUP)
    rev_r, rev_i = pr[t - 1::-1][:t], pi[t - 1::-1][:t]
    sre = rev_r[..., None] * bbr[None] - rev_i[..., None] * bbi[None]
    sim = rev_r[..., None] * bbi[None] + rev_i[..., None] * bbr[None]
    to_rows = lambda z: jnp.transpose(z, (1, 0, 3, 2)).reshape(SSM_NG, t * SSM_GROUP, SSM_STATE)
    sre, sim = to_rows(sre), to_rows(sim)
    bm = jnp.concatenate([sre, sim, sim, sre], axis=-1)
    to_cols = lambda z: jnp.transpose(z, (1, 3, 0, 2)).reshape(SSM_NG, SSM_STATE, t * SSM_GROUP)
    cm = jnp.concatenate([to_cols(car[1:]), -to_cols(cai[1:])], axis=1)
    a1 = jnp.concatenate([pr[t], pr[t]], axis=-1).reshape(1, SSM_NG * LANES)
    a2 = jnp.concatenate([-pi[t], pi[t]], axis=-1).reshape(1, SSM_NG * LANES)
    return m.astype(BF16), bm.astype(BF16), cm.astype(BF16), a1, a2


def _s5_mixer(proj, mats, d, w_glu, b_glu, gn, *, bsz, lp):
    m, bm, cm, a1, a2 = mats
    t = SSM_CHUNK
    cw = t * SSM_GROUP
    n_chunk = lp // t
    nc = bsz * n_chunk
    rows = bsz * lp
    u = proj[:, COL_SU * GROUP_W:(COL_SU + 1) * GROUP_W]
    u_c = jnp.transpose(u.reshape(nc, t, SSM_NG, SSM_GROUP), (0, 2, 1, 3)).reshape(nc, SSM_NG * cw)

    y_intra, s_end = pl.pallas_call(
        _s5_intra_kernel,
        grid=(SSM_NG,),
        in_specs=[
            pl.BlockSpec((nc, cw), lambda g: (0, g)),
            pl.BlockSpec((1, cw, cw), lambda g: (g, 0, 0)),
            pl.BlockSpec((1, cw, cw), lambda g: (g, 0, 0)),
        ],
        out_specs=[pl.BlockSpec((nc, cw), lambda g: (0, g)), pl.BlockSpec((nc, cw), lambda g: (0, g))],
        out_shape=[jax.ShapeDtypeStruct((nc, SSM_NG * cw), F32), jax.ShapeDtypeStruct((nc, SSM_NG * cw), F32)],
        compiler_params=_params("parallel"),
        name="s5_intra",
    )(u_c, m, bm)

    gb = 8
    x_start = pl.pallas_call(
        functools.partial(_s5_scan_kernel, n_chunk=n_chunk, bsz=bsz, groups=gb),
        grid=(SSM_NG // gb,),
        in_specs=[
            pl.BlockSpec((nc, gb * cw), lambda g: (0, g)),
            pl.BlockSpec((1, gb * LANES), lambda g: (0, g)),
            pl.BlockSpec((1, gb * LANES), lambda g: (0, g)),
        ],
        out_specs=pl.BlockSpec((nc, gb * LANES), lambda g: (0, g)),
        out_shape=jax.ShapeDtypeStruct((nc, SSM_NG * LANES), F32),
        compiler_params=_params("parallel"),
        name="s5_scan",
    )(s_end, a1, a2)

    y_c = pl.pallas_call(
        _s5_cross_kernel,
        grid=(SSM_NG,),
        in_specs=[
            pl.BlockSpec((nc, cw), lambda g: (0, g)),
            pl.BlockSpec((nc, LANES), lambda g: (0, g)),
            pl.BlockSpec((1, LANES, cw), lambda g: (g, 0, 0)),
        ],
        out_specs=pl.BlockSpec((nc, cw), lambda g: (0, g)),
        out_shape=jax.ShapeDtypeStruct((nc, SSM_NG * cw), F32),
        compiler_params=_params("parallel"),
        name="s5_cross",
    )(y_intra, x_start, cm)

    y = jnp.transpose(y_c.reshape(nc, SSM_NG, t, SSM_GROUP), (0, 2, 1, 3)).reshape(rows, GROUP_W)

    tm = SEQ_TILE
    vec = lambda: pl.BlockSpec((1, GROUP_W), lambda i: (0, 0))
    return pl.pallas_call(
        _s5_out_kernel,
        grid=(rows // tm,),
        in_specs=[
            pl.BlockSpec((tm, GROUP_W), lambda i: (i, 0)),
            pl.BlockSpec((tm, GROUP_W), lambda i: (i, COL_SU)),
            vec(),
            pl.BlockSpec((GROUP_W, GROUP_W), lambda i: (0, 0)),
            vec(), vec(),
        ],
        out_specs=pl.BlockSpec((tm, GROUP_W), lambda i: (i, 0)),
        out_shape=jax.ShapeDtypeStruct((rows, GROUP_W), BF16),
        compiler_params=_params("parallel"),
        name="s5_out",
    )(y, proj, d.reshape(1, GROUP_W), w_glu.astype(BF16), b_glu.reshape(1, GROUP_W), gn.reshape(1, GROUP_W))


def _retention_kernel(lg_ref, q_ref, k_ref, v_ref, g_ref, gn_ref, o_ref, st_scr, *, t):
    c = pl.program_id(2)

    @pl.when(c == 0)
    def _():
        st_scr[...] = jnp.zeros(st_scr.shape, F32)

    lg = lg_ref[0][:, :1]
    q, k, v = q_ref[...], k_ref[...], v_ref[...]
    ti = lax.broadcasted_iota(jnp.int32, (t, t), 0)
    si = lax.broadcasted_iota(jnp.int32, (t, t), 1)
    rel = (ti - si).astype(F32)
    dmat = jnp.where(ti >= si, jnp.exp(lg * jnp.maximum(rel, 0.0)), 0.0)
    col = lax.broadcasted_iota(jnp.int32, (t, 1), 0).astype(F32)
    q_decay = jnp.exp(lg * (col + 1.0))
    k_decay = jnp.exp(lg * (t - 1.0 - col))
    chunk_decay = jnp.exp(lg * float(t))

    s = lax.dot_general(q, k, (((1,), (1,)), ((), ())), preferred_element_type=F32) * dmat
    state = st_scr[...]
    o = (jnp.dot(s.astype(BF16), v, preferred_element_type=F32)
         + jnp.dot(q, state.astype(BF16), preferred_element_type=F32) * q_decay)
    kd = (k.astype(F32) * k_decay).astype(BF16)
    st_scr[...] = chunk_decay * state + lax.dot_general(kd, v, (((0,), (0,)), ((), ())),
                                                        preferred_element_type=F32)

    mu = jnp.mean(o, axis=-1, keepdims=True)
    oc = o - mu
    y = oc * lax.rsqrt(jnp.mean(oc * oc, axis=-1, keepdims=True) + EPS) * gn_ref[...]
    gate = g_ref[...].astype(F32)
    o_ref[...] = (y * (gate / (1.0 + jnp.exp(-gate)))).astype(BF16)


def _retention(proj, gn, *, bsz, lp):
    t = SEQ_TILE
    nt = lp // t
    log_gamma = jnp.log1p(-jnp.exp2(-5.0 - jnp.arange(RET_HEADS, dtype=F32)))
    lg = jnp.broadcast_to(log_gamma[:, None, None], (RET_HEADS, 1, LANES))
    blk = lambda off: pl.BlockSpec((t, LANES), lambda b, h, c: (b * nt + c, off * 4 + h))
    return pl.pallas_call(
        functools.partial(_retention_kernel, t=t),
        grid=(bsz, RET_HEADS, nt),
        in_specs=[
            pl.BlockSpec((1, 1, LANES), lambda b, h, c: (h, 0, 0)),
            blk(COL_RQ), blk(COL_RK), blk(COL_RV), blk(COL_RG),
            pl.BlockSpec((1, RET_DIM), lambda b, h, c: (0, 0)),
        ],
        out_specs=pl.BlockSpec((t, LANES), lambda b, h, c: (b * nt + c, h)),
        out_shape=jax.ShapeDtypeStruct((bsz * lp, GROUP_W), BF16),
        scratch_shapes=[pltpu.VMEM((RET_DIM, RET_DIM), F32)],
        compiler_params=_params("parallel", "parallel", "arbitrary"),
        name="retention",
    )(lg, proj, proj, proj, proj, gn.reshape(1, RET_DIM))


def _out_proj_kernel(a_ref, b_ref, c_ref, d_ref, w_ref, x_ref, g_ref, o_ref):
    acc = jnp.dot(a_ref[...], w_ref[0], preferred_element_type=F32)
    acc += jnp.dot(b_ref[...], w_ref[1], preferred_element_type=F32)
    acc += jnp.dot(c_ref[...], w_ref[2], preferred_element_type=F32)
    acc += jnp.dot(d_ref[...], w_ref[3], preferred_element_type=F32)
    y = acc * lax.rsqrt(jnp.mean(acc * acc, axis=-1, keepdims=True) + EPS) * g_ref[...]
    o_ref[...] = x_ref[...] + y


def _out_proj(mixes, w_bf16, x2d, gain):
    rows = x2d.shape[0]
    tm = MXU_DIM
    mix_spec = pl.BlockSpec((tm, GROUP_W), lambda i: (i, 0))
    return pl.pallas_call(
        _out_proj_kernel,
        grid=(rows // tm,),
        in_specs=[
            mix_spec, mix_spec, mix_spec, mix_spec,
            pl.BlockSpec((4, GROUP_W, D_MODEL), lambda i: (0, 0, 0)),
            pl.BlockSpec((tm, D_MODEL), lambda i: (i, 0)),
            pl.BlockSpec((1, D_MODEL), lambda i: (0, 0)),
        ],
        out_specs=pl.BlockSpec((tm, D_MODEL), lambda i: (i, 0)),
        out_shape=jax.ShapeDtypeStruct((rows, D_MODEL), F32),
        compiler_params=_params("parallel"),
        name="out_proj",
    )(*mixes, w_bf16.reshape(4, GROUP_W, D_MODEL), x2d, gain.reshape(1, D_MODEL))


def _ffn_kernel(x_ref, gpre_ref, wg_ref, wu_ref, wd_ref, gpost_ref, o_ref, h_scr):
    k = pl.program_id(1)

    @pl.when(k == 0)
    def _():
        x = x_ref[...]
        h_scr[...] = (x * lax.rsqrt(jnp.mean(x * x, axis=-1, keepdims=True) + EPS) * gpre_ref[...]).astype(BF16)

    h = h_scr[...]
    gate = jnp.dot(h, wg_ref[...], preferred_element_type=F32)
    up = jnp.dot(h, wu_ref[...], preferred_element_type=F32)
    act = (gate / (1.0 + jnp.exp(-gate)) * up).astype(BF16)
    part = jnp.dot(act, wd_ref[...], preferred_element_type=F32)

    @pl.when(k == 0)
    def _():
        o_ref[...] = part

    @pl.when(k > 0)
    def _():
        o_ref[...] += part

    @pl.when(k == pl.num_programs(1) - 1)
    def _():
        f = o_ref[...]
        o_ref[...] = x_ref[...] + f * lax.rsqrt(jnp.mean(f * f, axis=-1, keepdims=True) + EPS) * gpost_ref[...]


def _ffn(x2d, gpre, wg, wu, wd, gpost):
    rows = x2d.shape[0]
    tm = 2 * MXU_DIM
    tf = GROUP_W
    return pl.pallas_call(
        _ffn_kernel,
        grid=(rows // tm, FFN_HIDDEN // tf),
        in_specs=[
            pl.BlockSpec((tm, D_MODEL), lambda i, k: (i, 0)),
            pl.BlockSpec((1, D_MODEL), lambda i, k: (0, 0)),
            pl.BlockSpec((D_MODEL, tf), lambda i, k: (0, k)),
            pl.BlockSpec((D_MODEL, tf), lambda i, k: (0, k)),
            pl.BlockSpec((tf, D_MODEL), lambda i, k: (k, 0)),
            pl.BlockSpec((1, D_MODEL), lambda i, k: (0, 0)),
        ],
        out_specs=pl.BlockSpec((tm, D_MODEL), lambda i, k: (i, 0)),
        out_shape=jax.ShapeDtypeStruct((rows, D_MODEL), F32),
        scratch_shapes=[pltpu.VMEM((tm, D_MODEL), BF16)],
        compiler_params=_params("parallel", "arbitrary"),
        name="ffn",
    )(x2d, gpre.reshape(1, D_MODEL), wg, wu, wd, gpost.reshape(1, D_MODEL))


def kernel(x, meta_tokens, norm_mix_pre, w_in, diff_lambda_q1, diff_lambda_k1, diff_lambda_q2, diff_lambda_k2, diff_norm, sb_norm, ssm_a_re, ssm_a_im, ssm_log_dt, ssm_b_re, ssm_b_im, ssm_c_re, ssm_c_im, ssm_d, ssm_w_glu, ssm_b_glu, ssm_norm, ret_norm, w_out, norm_mix_post, norm_ffn_pre, w_ffn_gate, w_ffn_up, w_ffn_down, norm_ffn_post):
    bsz, seq, d_model = x.shape
    assert d_model == D_MODEL and meta_tokens.shape == (N_META, D_MODEL)
    depth = w_in.shape[0]
    n_tok = N_META + seq
    lp = _padded_len(n_tok)
    pad_front = lp - n_tok

    meta = jnp.broadcast_to(meta_tokens[None].astype(x.dtype), (bsz, N_META, D_MODEL))
    h = jnp.concatenate([jnp.zeros((bsz, pad_front, D_MODEL), x.dtype), meta, x], axis=1)
    h = h.reshape(bsz * lp, D_MODEL)

    tabs = _rotary_tables(lp, pad_front)
    tri = (jnp.arange(MXU_DIM)[:, None] >= jnp.arange(MXU_DIM)[None, :]).astype(BF16)

    for l in range(depth):
        lam_init = 0.8 - 0.6 * math.exp(-0.3 * l)
        proj = _in_proj(h, norm_mix_pre[l], w_in[l].astype(BF16), tabs, lp=lp, pad_front=pad_front)
        lam_vecs = jnp.stack([diff_lambda_q1[l], diff_lambda_k1[l], diff_lambda_q2[l], diff_lambda_k2[l]]).astype(F32)
        o_diff = _diff_attention(proj, lam_vecs, diff_norm[l], bsz=bsz, lp=lp, pad_front=pad_front, lam_init=lam_init)
        o_sb = _sb_attention(proj, tri, sb_norm[l], bsz=bsz, lp=lp, pad_front=pad_front)
        mats = _s5_matrices(ssm_a_re[l], ssm_a_im[l], ssm_log_dt[l], ssm_b_re[l], ssm_b_im[l], ssm_c_re[l], ssm_c_im[l])
        o_ssm = _s5_mixer(proj, mats, ssm_d[l], ssm_w_glu[l], ssm_b_glu[l], ssm_norm[l], bsz=bsz, lp=lp)
        o_ret = _retention(proj, ret_norm[l], bsz=bsz, lp=lp)
        h = _out_proj((o_diff, o_sb, o_ssm, o_ret), w_out[l].astype(BF16), h, norm_mix_post[l])
        h = _ffn(h, norm_ffn_pre[l], w_ffn_gate[l].astype(BF16), w_ffn_up[l].astype(BF16),
                 w_ffn_down[l].astype(BF16), norm_ffn_post[l])

    return h.reshape(bsz, lp, D_MODEL)[:, pad_front + N_META:]
```

```python
import functools
import math

import jax
import jax.numpy as jnp
from jax import lax
from jax.experimental import pallas as pl
from jax.experimental.pallas import tpu as pltpu

F32 = jnp.float32
BF16 = jnp.bfloat16

D_MODEL = 2048
N_META = 16
GROUP_W = D_MODEL // 4
DIFF_HEADS = 4
DIFF_QK = 64
DIFF_V = GROUP_W // DIFF_HEADS
SB_HEADS = 8
SB_DIM = GROUP_W // SB_HEADS
SSM_GROUP = 16
SSM_NG = GROUP_W // SSM_GROUP
SSM_STATE = 64
RET_HEADS = 4
RET_DIM = GROUP_W // RET_HEADS
FFN_HIDDEN = -(-8 * D_MODEL // (3 * 256)) * 256
IN_COLS = 11 * GROUP_W
ROPE_THETA = 500000.0
ROPE_DIM = DIFF_QK // 4
RET_THETA = 10000.0
NEG_INF = -1e30
EPS = 1e-6
LOG2E = 1.4426950408889634

LANES = 128
MXU_DIM = 256
SEQ_TILE = 768
SSM_CHUNK = 16
VMEM_LIMIT = 52 * 1024 * 1024

COL_DQ, COL_DK, COL_DV, COL_SQ, COL_SK, COL_SV, COL_SU, COL_RQ, COL_RK, COL_RV, COL_RG = range(11)


def _padded_len(n_tok):
    lp = -(-n_tok // MXU_DIM) * MXU_DIM
    while lp % SEQ_TILE:
        lp += MXU_DIM
    return lp


def _params(*sem):
    return pltpu.CompilerParams(dimension_semantics=sem, vmem_limit_bytes=VMEM_LIMIT)


def _in_proj_kernel(x_ref, g_ref, w_ref, dc_ref, dsa_ref, dsb_ref, rc_ref, rs_ref, o_ref, h_scr,
                    *, tm, tiles_per_batch, pad_front):
    i = pl.program_id(0)
    j = pl.program_id(1)

    @pl.when(j == 0)
    def _():
        x = x_ref[...]
        y = x * lax.rsqrt(jnp.mean(x * x, axis=-1, keepdims=True) + EPS) * g_ref[...]
        pos = (i % tiles_per_batch) * tm + lax.broadcasted_iota(jnp.int32, (tm, 1), 0)
        h_scr[...] = jnp.where(pos >= pad_front, y, 0.0).astype(BF16)

    acc = jnp.dot(h_scr[...], w_ref[...], preferred_element_type=F32)

    is_diff = (j == COL_DQ) | (j == COL_DK)
    is_ret = (j == COL_RQ) | (j == COL_RK)

    @pl.when(is_diff)
    def _():
        scale = jnp.where(j == COL_DQ, LOG2E * DIFF_QK ** -0.5, 1.0).astype(F32)
        c, sa, sb = dc_ref[...], dsa_ref[...], dsb_ref[...]
        for n in range(GROUP_W // LANES):
            a = acc[:, n * LANES:(n + 1) * LANES]
            r = a * c + pltpu.roll(a, LANES - ROPE_DIM // 2, 1) * sa + pltpu.roll(a, ROPE_DIM // 2, 1) * sb
            o_ref[:, n * LANES:(n + 1) * LANES] = (r * scale).astype(BF16)

    @pl.when(is_ret)
    def _():
        scale = jnp.where(j == COL_RK, RET_DIM ** -0.5, 1.0).astype(F32)
        c, s = rc_ref[...], rs_ref[...]
        for n in range(GROUP_W // LANES):
            a = acc[:, n * LANES:(n + 1) * LANES]
            r = a * c + pltpu.roll(a, RET_DIM // 2, 1) * s
            o_ref[:, n * LANES:(n + 1) * LANES] = (r * scale).astype(BF16)

    @pl.when(jnp.logical_not(is_diff | is_ret))
    def _():
        scale = jnp.where(j == COL_SQ, LOG2E * SB_DIM ** -0.5, 1.0).astype(F32)
        o_ref[...] = (acc * scale).astype(BF16)


def _in_proj(x2d, gain, w_bf16, tabs, *, lp, pad_front):
    rows = x2d.shape[0]
    tm = SEQ_TILE
    tiles_per_batch = lp // tm
    n_col = IN_COLS // GROUP_W
    tab_spec = pl.BlockSpec((tm, LANES), lambda i, j: (i % tiles_per_batch, 0))
    return pl.pallas_call(
        functools.partial(_in_proj_kernel, tm=tm, tiles_per_batch=tiles_per_batch, pad_front=pad_front),
        grid=(rows // tm, n_col),
        in_specs=[
            pl.BlockSpec((tm, D_MODEL), lambda i, j: (i, 0)),
            pl.BlockSpec((1, D_MODEL), lambda i, j: (0, 0)),
            pl.BlockSpec((D_MODEL, GROUP_W), lambda i, j: (0, j)),
            tab_spec, tab_spec, tab_spec, tab_spec, tab_spec,
        ],
        out_specs=pl.BlockSpec((tm, GROUP_W), lambda i, j: (i, j)),
        out_shape=jax.ShapeDtypeStruct((rows, IN_COLS), BF16),
        scratch_shapes=[pltpu.VMEM((tm, D_MODEL), BF16)],
        compiler_params=_params("parallel", "arbitrary"),
        name="in_proj",
    )(x2d, gain.reshape(1, D_MODEL), w_bf16, *tabs)


def _rotary_tables(lp, pad_front):
    pos = (jnp.arange(lp, dtype=jnp.int32) - pad_front).astype(F32)[:, None]
    lane = jnp.arange(LANES)
    half = ROPE_DIM // 2
    inv = 1.0 / (ROPE_THETA ** (jnp.arange(half, dtype=F32) * (2.0 / ROPE_DIM)))
    ang = pos * inv[None, :]
    d = lane % DIFF_QK
    cos_l = jnp.cos(ang)[:, d % half]
    sin_l = jnp.sin(ang)[:, d % half]
    dc = jnp.where(d < ROPE_DIM, cos_l, 1.0)
    dsa = jnp.where(d < half, -sin_l, 0.0)
    dsb = jnp.where((d >= half) & (d < ROPE_DIM), sin_l, 0.0)
    rhalf = RET_DIM // 2
    rinv = 1.0 / (RET_THETA ** (jnp.arange(rhalf, dtype=F32) * (2.0 / RET_DIM)))
    rang = pos * rinv[None, :]
    r = lane % RET_DIM
    rc = jnp.cos(rang)[:, r % rhalf]
    rs = jnp.where(r < rhalf, -1.0, 1.0) * jnp.sin(rang)[:, r % rhalf]
    return dc, dsa, dsb, rc, rs


def _tile_pairs(nt, descending):
    pairs = [(i, j) for i in range(nt) for j in (range(i, -1, -1) if descending else range(i + 1))]
    return jnp.array(list(zip(*pairs)), dtype=jnp.int32)


def _dispatch_by_mask_kind(i, j, pad_front, step):
    has_pad = pad_front > 0

    @pl.when((j != i) & (j != 0))
    def _():
        step(causal=False, pad=False)

    @pl.when((j == i) & (i != 0))
    def _():
        step(causal=True, pad=False)

    @pl.when((j == 0) & (i != 0))
    def _():
        step(causal=False, pad=has_pad)

    @pl.when(i == 0)
    def _():
        step(causal=True, pad=has_pad)


def _diff_attn_kernel(ij_ref, lam_ref, q_ref, k_ref, v_ref, gn_ref, o_ref, m_scr, l_scr, acc_scr,
                      *, t, pad_front, lam_init):
    pair = pl.program_id(2)
    i = ij_ref[0, pair]
    j = ij_ref[1, pair]

    @pl.when(j == 0)
    def _():
        m_scr[...] = jnp.full(m_scr.shape, NEG_INF, F32)
        l_scr[...] = jnp.zeros(l_scr.shape, F32)
        acc_scr[...] = jnp.zeros(acc_scr.shape, F32)

    def step(causal, pad):
        q, k, v = q_ref[...], k_ref[...], v_ref[...]
        v_ext = jnp.concatenate([v, jnp.ones_like(v)], axis=1)
        mask = None
        if causal or pad:
            row = lax.broadcasted_iota(jnp.int32, (t, t), 0)
            col = lax.broadcasted_iota(jnp.int32, (t, t), 1)
            if causal:
                mask = col <= row
            if pad:
                mask = (col >= pad_front) if mask is None else mask & (col >= pad_front)
        lane = lax.broadcasted_iota(jnp.int32, q.shape, 1)
        for c in range(2):
            qc = jnp.where((lane >= c * DIFF_QK) & (lane < (c + 1) * DIFF_QK), q, jnp.zeros_like(q))
            s = lax.dot_general(qc, k, (((1,), (1,)), ((), ())), preferred_element_type=F32)
            if mask is not None:
                s = jnp.where(mask, s, NEG_INF)
            m_prev = m_scr[c]
            m_new = jnp.maximum(m_prev, jnp.max(s, axis=-1, keepdims=True))
            alpha = jnp.exp2(m_prev - m_new)
            p = jnp.exp2(s - pltpu.repeat(m_new, t // LANES, axis=1))
            pv = jnp.dot(p.astype(BF16), v_ext, preferred_element_type=F32)
            acc_scr[c] = alpha * acc_scr[c] + pv[:, :DIFF_V]
            l_scr[c] = alpha * l_scr[c] + pv[:, DIFF_V:]
            m_scr[c] = m_new

    _dispatch_by_mask_kind(i, j, pad_front, step)

    @pl.when(j == i)
    def _():
        lv = lam_ref[...]
        lam = (jnp.exp(jnp.sum(lv[0:1] * lv[1:2], axis=-1, keepdims=True))
               - jnp.exp(jnp.sum(lv[2:3] * lv[3:4], axis=-1, keepdims=True)) + lam_init)
        o = acc_scr[0] / l_scr[0] - lam * (acc_scr[1] / l_scr[1])
        y = o * lax.rsqrt(jnp.mean(o * o, axis=-1, keepdims=True) + EPS) * gn_ref[...]
        o_ref[...] = (y * (1.0 - lam_init)).astype(BF16)


def _diff_attention(proj, lam_vecs, gn, *, bsz, lp, pad_front, lam_init):
    t = SEQ_TILE
    nt = lp // t
    ij = _tile_pairs(nt, descending=False)
    grid_spec = pltpu.PrefetchScalarGridSpec(
        num_scalar_prefetch=1,
        grid=(bsz, DIFF_HEADS, ij.shape[1]),
        in_specs=[
            pl.BlockSpec((4, DIFF_QK), lambda b, h, p, ij: (0, 0)),
            pl.BlockSpec((t, LANES), lambda b, h, p, ij: (b * nt + ij[0, p], COL_DQ * 4 + h)),
            pl.BlockSpec((t, LANES), lambda b, h, p, ij: (b * nt + ij[1, p], COL_DK * 4 + h)),
            pl.BlockSpec((t, LANES), lambda b, h, p, ij: (b * nt + ij[1, p], COL_DV * 4 + h)),
            pl.BlockSpec((1, DIFF_V), lambda b, h, p, ij: (0, 0)),
        ],
        out_specs=pl.BlockSpec((t, LANES), lambda b, h, p, ij: (b * nt + ij[0, p], h)),
        scratch_shapes=[pltpu.VMEM((2, t, LANES), F32), pltpu.VMEM((2, t, LANES), F32),
                        pltpu.VMEM((2, t, DIFF_V), F32)],
    )
    return pl.pallas_call(
        functools.partial(_diff_attn_kernel, t=t, pad_front=pad_front, lam_init=lam_init),
        grid_spec=grid_spec,
        out_shape=jax.ShapeDtypeStruct((bsz * lp, GROUP_W), BF16),
        compiler_params=_params("parallel", "parallel", "arbitrary"),
        name="diff_attn",
    )(ij, lam_vecs, proj, proj, proj, gn.reshape(1, DIFF_V))


def _sb_attn_kernel(ij_ref, q_ref, k_ref, v_ref, tri_ref, gn_ref, o_ref, run_scr, acc_scr,
                    *, t, sub, pad_front):
    pair = pl.program_id(2)
    i = ij_ref[0, pair]
    j = ij_ref[1, pair]

    @pl.when(j == i)
    def _():
        run_scr[...] = jnp.zeros(run_scr.shape, F32)
        acc_scr[...] = jnp.zeros(acc_scr.shape, F32)

    def block(qh, hh, r0, c0, mask):
        z = lax.dot_general(qh[r0:], k_ref[c0:c0 + sub, :], (((1,), (1,)), ((), ())),
                            preferred_element_type=F32)
        sign_bit = jnp.uint32(0x80000000)
        neg_abs = lax.bitcast_convert_type(lax.bitcast_convert_type(z, jnp.uint32) | sign_bit, F32)
        sp = jnp.maximum(z, 0.0) + jnp.log(1.0 + jnp.exp2(neg_abs)) * LOG2E
        if mask is not None:
            sp = jnp.where(mask, sp, 0.0)
        tail = jnp.dot(sp.astype(BF16), tri_ref[...], preferred_element_type=F32)
        run = run_scr[hh, r0:, :]
        log_w = z - tail - pltpu.repeat(run, sub // LANES, axis=1)
        if mask is not None:
            log_w = jnp.where(mask, log_w, NEG_INF)
        w = jnp.exp2(log_w)
        acc_scr[hh, r0:, :] += jnp.dot(w.astype(BF16), v_ref[c0:c0 + sub, :], preferred_element_type=F32)
        run_scr[hh, r0:, :] = run + jnp.sum(sp, axis=-1, keepdims=True)

    def step(causal, pad):
        q = q_ref[...]
        lane = lax.broadcasted_iota(jnp.int32, q.shape, 1)
        first_key = pad_front if pad else 0
        for hh in range(LANES // SB_DIM):
            qh = jnp.where((lane >= hh * SB_DIM) & (lane < (hh + 1) * SB_DIM), q, jnp.zeros_like(q))
            for c0 in reversed(range(0, t, sub)):
                if c0 + sub <= first_key:
                    continue
                r0 = c0 if causal else 0
                mask = None
                if causal or c0 < first_key:
                    row = r0 + lax.broadcasted_iota(jnp.int32, (t - r0, sub), 0)
                    col = c0 + lax.broadcasted_iota(jnp.int32, (t - r0, sub), 1)
                    if causal:
                        mask = col < row
                    if c0 < first_key:
                        mask = (col >= first_key) if mask is None else mask & (col >= first_key)
                block(qh, hh, r0, c0, mask)

    _dispatch_by_mask_kind(i, j, pad_front, step)

    @pl.when(j == 0)
    def _():
        lane = lax.broadcasted_iota(jnp.int32, (t, LANES), 1)
        y = jnp.zeros((t, LANES), F32)
        for hh in range(LANES // SB_DIM):
            in_head = (lane >= hh * SB_DIM) & (lane < (hh + 1) * SB_DIM)
            o = jnp.where(in_head, acc_scr[hh], 0.0)
            ms = jnp.sum(o * o, axis=-1, keepdims=True) * (1.0 / SB_DIM)
            y = y + o * lax.rsqrt(ms + EPS)
        o_ref[...] = (y * gn_ref[...]).astype(BF16)


def _sb_attention(proj, tri, gn, *, bsz, lp, pad_front):
    t = SEQ_TILE
    nt = lp // t
    n_pair = GROUP_W // LANES
    ij = _tile_pairs(nt, descending=True)
    grid_spec = pltpu.PrefetchScalarGridSpec(
        num_scalar_prefetch=1,
        grid=(bsz, n_pair, ij.shape[1]),
        in_specs=[
            pl.BlockSpec((t, LANES), lambda b, h, p, ij: (b * nt + ij[0, p], COL_SQ * 4 + h)),
            pl.BlockSpec((t, LANES), lambda b, h, p, ij: (b * nt + ij[1, p], COL_SK * 4 + h)),
            pl.BlockSpec((t, LANES), lambda b, h, p, ij: (b * nt + ij[1, p], COL_SV * 4 + h)),
            pl.BlockSpec((MXU_DIM, MXU_DIM), lambda b, h, p, ij: (0, 0)),
            pl.BlockSpec((1, LANES), lambda b, h, p, ij: (0, 0)),
        ],
        out_specs=pl.BlockSpec((t, LANES), lambda b, h, p, ij: (b * nt + ij[0, p], h)),
        scratch_shapes=[pltpu.VMEM((2, t, LANES), F32), pltpu.VMEM((2, t, LANES), F32)],
    )
    return pl.pallas_call(
        functools.partial(_sb_attn_kernel, t=t, sub=MXU_DIM, pad_front=pad_front),
        grid_spec=grid_spec,
        out_shape=jax.ShapeDtypeStruct((bsz * lp, GROUP_W), BF16),
        compiler_params=_params("parallel", "parallel", "arbitrary"),
        name="sb_attn",
    )(ij, proj, proj, proj, tri, jnp.tile(gn.reshape(1, SB_DIM), (1, LANES // SB_DIM)))


def _s5_intra_kernel(u_ref, m_ref, bm_ref, y_ref, s_ref):
    u = u_ref[...]
    y_ref[...] = jnp.dot(u, m_ref[0], preferred_element_type=F32)
    s_ref[...] = jnp.dot(u, bm_ref[0], preferred_element_type=F32)


def _s5_scan_kernel(s_ref, a1_ref, a2_ref, x_ref, *, n_chunk, bsz, groups):
    a1 = a1_ref[...]
    a2 = a2_ref[...]
    w = groups * LANES

    def body(c, carry):
        new = []
        for b in range(bsz):
            x, xs = carry[b]
            row = s_ref[pl.ds(b * n_chunk + c, 1), :]
            x_ref[pl.ds(b * n_chunk + c, 1), :] = x
            s_x = jnp.concatenate([row[:, g * 2 * LANES:g * 2 * LANES + LANES] for g in range(groups)], axis=1)
            s_xs = jnp.concatenate([row[:, g * 2 * LANES + LANES:(g + 1) * 2 * LANES] for g in range(groups)], axis=1)
            new.append((a1 * x + a2 * xs + s_x, a1 * xs - a2 * x + s_xs))
        return tuple(new)

    zero = jnp.zeros((1, w), F32)
    lax.fori_loop(0, n_chunk, body, tuple((zero, zero) for _ in range(bsz)))


def _s5_cross_kernel(yi_ref, x_ref, cm_ref, y_ref):
    y_ref[...] = yi_ref[...] + jnp.dot(x_ref[...].astype(BF16), cm_ref[0], preferred_element_type=F32)


def _s5_out_kernel(y_ref, u_ref, d_ref, w_ref, b_ref, gn_ref, o_ref):
    y = y_ref[...] + d_ref[...] * u_ref[...].astype(F32)
    g = 0.5 * y * (1.0 + jnp.tanh(math.sqrt(2.0 / math.pi) * (y + 0.044715 * (y * y * y))))
    gate = jnp.dot(g.astype(BF16), w_ref[...], preferred_element_type=F32) + b_ref[...]
    o = g * (1.0 / (1.0 + jnp.exp(-gate)))
    o_ref[...] = (o * lax.rsqrt(jnp.mean(o * o, axis=-1, keepdims=True) + EPS) * gn_ref[...]).astype(BF16)


def _s5_matrices(a_re, a_im, log_dt, b_re, b_im, c_re, c_im):
    hp = lax.Precision.HIGHEST
    t = SSM_CHUNK
    dt = jnp.exp(log_dt.astype(F32))[:, None]
    ar, ai = a_re.astype(F32), a_im.astype(F32)
    dpow = jnp.arange(t + 1, dtype=F32)[:, None, None]
    mag = jnp.exp(dpow * (ar * dt)[None])
    pr = mag * jnp.cos(dpow * (ai * dt)[None])
    pi = mag * jnp.sin(dpow * (ai * dt)[None])
    nr, ni = pr[1] - 1.0, pi[1]
    den = ar * ar + ai * ai
    fr = (nr * ar + ni * ai) / den
    fi = (ni * ar - nr * ai) / den
    br, bi = b_re.astype(F32), b_im.astype(F32)
    bbr = fr[..., None] * br - fi[..., None] * bi
    bbi = fr[..., None] * bi + fi[..., None] * br
    cr, ci = c_re.astype(F32), c_im.astype(F32)
    car = cr[None] * pr[:, :, None, :] - ci[None] * pi[:, :, None, :]
    cai = cr[None] * pi[:, :, None, :] + ci[None] * pr[:, :, None, :]
    kern = (jnp.einsum('dghp,gpk->dghk', car[:t], bbr, precision=hp)
            - jnp.einsum('dghp,gpk->dghk', cai[:t], bbi, precision=hp))
    lag = jnp.arange(t)[None, :] - jnp.arange(t)[:, None]
    m = jnp.where((lag >= 0)[:, :, None, None, None], kern[jnp.maximum(lag, 0)], 0.0)
    m = jnp.transpose(m, (2, 0, 4, 1, 3)).reshape(SSM_NG, t * SSM_GROUP, t * SSM_GROUP)
    rev_r, rev_i = pr[t - 1::-1][:t], pi[t - 1::-1][:t]
    sre = rev_r[..., None] * bbr[None] - rev_i[..., None] * bbi[None]
    sim = rev_r[..., None] * bbi[None] + rev_i[..., None] * bbr[None]
    to_rows = lambda z: jnp.transpose(z, (1, 0, 3, 2)).reshape(SSM_NG, t * SSM_GROUP, SSM_STATE)
    sre, sim = to_rows(sre), to_rows(sim)
    bm = jnp.concatenate([sre, sim, sim, sre], axis=-1)
    to_cols = lambda z: jnp.transpose(z, (1, 3, 0, 2)).reshape(SSM_NG, SSM_STATE, t * SSM_GROUP)
    cm = jnp.concatenate([to_cols(car[1:]), -to_cols(cai[1:])], axis=1)
    a1 = jnp.concatenate([pr[t], pr[t]], axis=-1).reshape(1, SSM_NG * LANES)
    a2 = jnp.concatenate([-pi[t], pi[t]], axis=-1).reshape(1, SSM_NG * LANES)
    return m.astype(BF16), bm.astype(BF16), cm.astype(BF16), a1, a2


def _s5_mixer(proj, mats, d, w_glu, b_glu, gn, *, bsz, lp):
    m, bm, cm, a1, a2 = mats
    t = SSM_CHUNK
    cw = t * SSM_GROUP
    n_chunk = lp // t
    nc = bsz * n_chunk
    rows = bsz * lp
    u = proj[:, COL_SU * GROUP_W:(COL_SU + 1) * GROUP_W]
    u_c = jnp.transpose(u.reshape(nc, t, SSM_NG, SSM_GROUP), (0, 2, 1, 3)).reshape(nc, SSM_NG * cw)

    y_intra, s_end = pl.pallas_call(
        _s5_intra_kernel,
        grid=(SSM_NG,),
        in_specs=[
            pl.BlockSpec((nc, cw), lambda g: (0, g)),
            pl.BlockSpec((1, cw, cw), lambda g: (g, 0, 0)),
            pl.BlockSpec((1, cw, cw), lambda g: (g, 0, 0)),
        ],
        out_specs=[pl.BlockSpec((nc, cw), lambda g: (0, g)), pl.BlockSpec((nc, cw), lambda g: (0, g))],
        out_shape=[jax.ShapeDtypeStruct((nc, SSM_NG * cw), F32), jax.ShapeDtypeStruct((nc, SSM_NG * cw), F32)],
        compiler_params=_params("parallel"),
        name="s5_intra",
    )(u_c, m, bm)

    gb = 8
    x_start = pl.pallas_call(
        functools.partial(_s5_scan_kernel, n_chunk=n_chunk, bsz=bsz, groups=gb),
        grid=(SSM_NG // gb,),
        in_specs=[
            pl.BlockSpec((nc, gb * cw), lambda g: (0, g)),
            pl.BlockSpec((1, gb * LANES), lambda g: (0, g)),
            pl.BlockSpec((1, gb * LANES), lambda g: (0, g)),
        ],
        out_specs=pl.BlockSpec((nc, gb * LANES), lambda g: (0, g)),
        out_shape=jax.ShapeDtypeStruct((nc, SSM_NG * LANES), F32),
        compiler_params=_params("parallel"),
        name="s5_scan",
    )(s_end, a1, a2)

    y_c = pl.pallas_call(
        _s5_cross_kernel,
        grid=(SSM_NG,),
        in_specs=[
            pl.BlockSpec((nc, cw), lambda g: (0, g)),
            pl.BlockSpec((nc, LANES), lambda g: (0, g)),
            pl.BlockSpec((1, LANES, cw), lambda g: (g, 0, 0)),
        ],
        out_specs=pl.BlockSpec((nc, cw), lambda g: (0, g)),
        out_shape=jax.ShapeDtypeStruct((nc, SSM_NG * cw), F32),
        compiler_params=_params("parallel"),
        name="s5_cross",
    )(y_intra, x_start, cm)

    y = jnp.transpose(y_c.reshape(nc, SSM_NG, t, SSM_GROUP), (0, 2, 1, 3)).reshape(rows, GROUP_W)

    tm = SEQ_TILE
    vec = lambda: pl.BlockSpec((1, GROUP_W), lambda i: (0, 0))
    return pl.pallas_call(
        _s5_out_kernel,
        grid=(rows // tm,),
        in_specs=[
            pl.BlockSpec((tm, GROUP_W), lambda i: (i, 0)),
            pl.BlockSpec((tm, GROUP_W), lambda i: (i, COL_SU)),
            vec(),
            pl.BlockSpec((GROUP_W, GROUP_W), lambda i: (0, 0)),
            vec(), vec(),
        ],
        out_specs=pl.BlockSpec((tm, GROUP_W), lambda i: (i, 0)),
        out_shape=jax.ShapeDtypeStruct((rows, GROUP_W), BF16),
        compiler_params=_params("parallel"),
        name="s5_out",
    )(y, proj, d.reshape(1, GROUP_W), w_glu.astype(BF16), b_glu.reshape(1, GROUP_W), gn.reshape(1, GROUP_W))


def _retention_kernel(lg_ref, q_ref, k_ref, v_ref, g_ref, gn_ref, o_ref, st_scr, *, t):
    c = pl.program_id(2)

    @pl.when(c == 0)
    def _():
        st_scr[...] = jnp.zeros(st_scr.shape, F32)

    lg = lg_ref[0][:, :1]
    q, k, v = q_ref[...], k_ref[...], v_ref[...]
    ti = lax.broadcasted_iota(jnp.int32, (t, t), 0)
    si = lax.broadcasted_iota(jnp.int32, (t, t), 1)
    rel = (ti - si).astype(F32)
    dmat = jnp.where(ti >= si, jnp.exp(lg * jnp.maximum(rel, 0.0)), 0.0)
    col = lax.broadcasted_iota(jnp.int32, (t, 1), 0).astype(F32)
    q_decay = jnp.exp(lg * (col + 1.0))
    k_decay = jnp.exp(lg * (t - 1.0 - col))
    chunk_decay = jnp.exp(lg * float(t))

    s = lax.dot_general(q, k, (((1,), (1,)), ((), ())), preferred_element_type=F32) * dmat
    state = st_scr[...]
    o = (jnp.dot(s.astype(BF16), v, preferred_element_type=F32)
         + jnp.dot(q, state.astype(BF16), preferred_element_type=F32) * q_decay)
    kd = (k.astype(F32) * k_decay).astype(BF16)
    st_scr[...] = chunk_decay * state + lax.dot_general(kd, v, (((0,), (0,)), ((), ())),
                                                        preferred_element_type=F32)

    mu = jnp.mean(o, axis=-1, keepdims=True)
    oc = o - mu
    y = oc * lax.rsqrt(jnp.mean(oc * oc, axis=-1, keepdims=True) + EPS) * gn_ref[...]
    gate = g_ref[...].astype(F32)
    o_ref[...] = (y * (gate / (1.0 + jnp.exp(-gate)))).astype(BF16)


def _retention(proj, gn, *, bsz, lp):
    t = SEQ_TILE
    nt = lp // t
    log_gamma = jnp.log1p(-jnp.exp2(-5.0 - jnp.arange(RET_HEADS, dtype=F32)))
    lg = jnp.broadcast_to(log_gamma[:, None, None], (RET_HEADS, 1, LANES))
    blk = lambda off: pl.BlockSpec((t, LANES), lambda b, h, c: (b * nt + c, off * 4 + h))
    return pl.pallas_call(
        functools.partial(_retention_kernel, t=t),
        grid=(bsz, RET_HEADS, nt),
        in_specs=[
            pl.BlockSpec((1, 1, LANES), lambda b, h, c: (h, 0, 0)),
            blk(COL_RQ), blk(COL_RK), blk(COL_RV), blk(COL_RG),
            pl.BlockSpec((1, RET_DIM), lambda b, h, c: (0, 0)),
        ],
        out_specs=pl.BlockSpec((t, LANES), lambda b, h, c: (b * nt + c, h)),
        out_shape=jax.ShapeDtypeStruct((bsz * lp, GROUP_W), BF16),
        scratch_shapes=[pltpu.VMEM((RET_DIM, RET_DIM), F32)],
        compiler_params=_params("parallel", "parallel", "arbitrary"),
        name="retention",
    )(lg, proj, proj, proj, proj, gn.reshape(1, RET_DIM))


def _out_proj_kernel(a_ref, b_ref, c_ref, d_ref, w_ref, x_ref, g_ref, o_ref):
    acc = jnp.dot(a_ref[...], w_ref[0], preferred_element_type=F32)
    acc += jnp.dot(b_ref[...], w_ref[1], preferred_element_type=F32)
    acc += jnp.dot(c_ref[...], w_ref[2], preferred_element_type=F32)
    acc += jnp.dot(d_ref[...], w_ref[3], preferred_element_type=F32)
    y = acc * lax.rsqrt(jnp.mean(acc * acc, axis=-1, keepdims=True) + EPS) * g_ref[...]
    o_ref[...] = x_ref[...] + y


def _out_proj(mixes, w_bf16, x2d, gain):
    rows = x2d.shape[0]
    tm = MXU_DIM
    mix_spec = pl.BlockSpec((tm, GROUP_W), lambda i: (i, 0))
    return pl.pallas_call(
        _out_proj_kernel,
        grid=(rows // tm,),
        in_specs=[
            mix_spec, mix_spec, mix_spec, mix_spec,
            pl.BlockSpec((4, GROUP_W, D_MODEL), lambda i: (0, 0, 0)),
            pl.BlockSpec((tm, D_MODEL), lambda i: (i, 0)),
            pl.BlockSpec((1, D_MODEL), lambda i: (0, 0)),
        ],
        out_specs=pl.BlockSpec((tm, D_MODEL), lambda i: (i, 0)),
        out_shape=jax.ShapeDtypeStruct((rows, D_MODEL), F32),
        compiler_params=_params("parallel"),
        name="out_proj",
    )(*mixes, w_bf16.reshape(4, GROUP_W, D_MODEL), x2d, gain.reshape(1, D_MODEL))


def _ffn_kernel(x_ref, gpre_ref, wg_ref, wu_ref, wd_ref, gpost_ref, o_ref, h_scr):
    k = pl.program_id(1)

    @pl.when(k == 0)
    def _():
        x = x_ref[...]
        h_scr[...] = (x * lax.rsqrt(jnp.mean(x * x, axis=-1, keepdims=True) + EPS) * gpre_ref[...]).astype(BF16)

    h = h_scr[...]
    gate = jnp.dot(h, wg_ref[...], preferred_element_type=F32)
    up = jnp.dot(h, wu_ref[...], preferred_element_type=F32)
    act = (gate / (1.0 + jnp.exp(-gate)) * up).astype(BF16)
    part = jnp.dot(act, wd_ref[...], preferred_element_type=F32)

    @pl.when(k == 0)
    def _():
        o_ref[...] = part

    @pl.when(k > 0)
    def _():
        o_ref[...] += part

    @pl.when(k == pl.num_programs(1) - 1)
    def _():
        f = o_ref[...]
        o_ref[...] = x_ref[...] + f * lax.rsqrt(jnp.mean(f * f, axis=-1, keepdims=True) + EPS) * gpost_ref[...]


def _ffn(x2d, gpre, wg, wu, wd, gpost):
    rows = x2d.shape[0]
    tm = 2 * MXU_DIM
    tf = GROUP_W
    return pl.pallas_call(
        _ffn_kernel,
        grid=(rows // tm, FFN_HIDDEN // tf),
        in_specs=[
            pl.BlockSpec((tm, D_MODEL), lambda i, k: (i, 0)),
            pl.BlockSpec((1, D_MODEL), lambda i, k: (0, 0)),
            pl.BlockSpec((D_MODEL, tf), lambda i, k: (0, k)),
            pl.BlockSpec((D_MODEL, tf), lambda i, k: (0, k)),
            pl.BlockSpec((tf, D_MODEL), lambda i, k: (k, 0)),
            pl.BlockSpec((1, D_MODEL), lambda i, k: (0, 0)),
        ],
        out_specs=pl.BlockSpec((tm, D_MODEL), lambda i, k: (i, 0)),
        out_shape=jax.ShapeDtypeStruct((rows, D_MODEL), F32),
        scratch_shapes=[pltpu.VMEM((tm, D_MODEL), BF16)],
        compiler_params=_params("parallel", "arbitrary"),
        name="ffn",
    )(x2d, gpre.reshape(1, D_MODEL), wg, wu, wd, gpost.reshape(1, D_MODEL))


def kernel(x, meta_tokens, norm_mix_pre, w_in, diff_lambda_q1, diff_lambda_k1, diff_lambda_q2, diff_lambda_k2, diff_norm, sb_norm, ssm_a_re, ssm_a_im, ssm_log_dt, ssm_b_re, ssm_b_im, ssm_c_re, ssm_c_im, ssm_d, ssm_w_glu, ssm_b_glu, ssm_norm, ret_norm, w_out, norm_mix_post, norm_ffn_pre, w_ffn_gate, w_ffn_up, w_ffn_down, norm_ffn_post):
    bsz, seq, d_model = x.shape
    assert d_model == D_MODEL and meta_tokens.shape == (N_META, D_MODEL)
    depth = w_in.shape[0]
    n_tok = N_META + seq
    lp = _padded_len(n_tok)
    pad_front = lp - n_tok

    meta = jnp.broadcast_to(meta_tokens[None].astype(x.dtype), (bsz, N_META, D_MODEL))
    h = jnp.concatenate([jnp.zeros((bsz, pad_front, D_MODEL), x.dtype), meta, x], axis=1)
    h = h.reshape(bsz * lp, D_MODEL)

    tabs = _rotary_tables(lp, pad_front)
    tri = (jnp.arange(MXU_DIM)[:, None] >= jnp.arange(MXU_DIM)[None, :]).astype(BF16)

    for l in range(depth):
        lam_init = 0.8 - 0.6 * math.exp(-0.3 * l)
        proj = _in_proj(h, norm_mix_pre[l], w_in[l].astype(BF16), tabs, lp=lp, pad_front=pad_front)
        lam_vecs = jnp.stack([diff_lambda_q1[l], diff_lambda_k1[l], diff_lambda_q2[l], diff_lambda_k2[l]]).astype(F32)
        o_diff = _diff_attention(proj, lam_vecs, diff_norm[l], bsz=bsz, lp=lp, pad_front=pad_front, lam_init=lam_init)
        o_sb = _sb_attention(proj, tri, sb_norm[l], bsz=bsz, lp=lp, pad_front=pad_front)
        mats = _s5_matrices(ssm_a_re[l], ssm_a_im[l], ssm_log_dt[l], ssm_b_re[l], ssm_b_im[l], ssm_c_re[l], ssm_c_im[l])
        o_ssm = _s5_mixer(proj, mats, ssm_d[l], ssm_w_glu[l], ssm_b_glu[l], ssm_norm[l], bsz=bsz, lp=lp)
        o_ret = _retention(proj, ret_norm[l], bsz=bsz, lp=lp)
        h = _out_proj((o_diff, o_sb, o_ssm, o_ret), w_out[l].astype(BF16), h, norm_mix_post[l])
        h = _ffn(h, norm_ffn_pre[l], w_ffn_gate[l].astype(BF16), w_ffn_up[l].astype(BF16),
                 w_ffn_down[l].astype(BF16), norm_ffn_post[l])

    return h.reshape(bsz, lp, D_MODEL)[:, pad_front + N_META:]
```

```python
import functools
import math

import jax
import jax.numpy as jnp
from jax import lax
from jax.experimental import pallas as pl
from jax.experimental.pallas import tpu as pltpu

F32 = jnp.float32
BF16 = jnp.bfloat16

D_MODEL = 2048
N_META = 16
GROUP_W = D_MODEL // 4
DIFF_HEADS = 4
DIFF_QK = 64
DIFF_V = GROUP_W // DIFF_HEADS
SB_HEADS = 8
SB_DIM = GROUP_W // SB_HEADS
SSM_GROUP = 16
SSM_NG = GROUP_W // SSM_GROUP
SSM_STATE = 64
RET_HEADS = 4
RET_DIM = GROUP_W // RET_HEADS
FFN_HIDDEN = -(-8 * D_MODEL // (3 * 256)) * 256
IN_COLS = 11 * GROUP_W
ROPE_THETA = 500000.0
ROPE_DIM = DIFF_QK // 4
RET_THETA = 10000.0
NEG_INF = -1e30
EPS = 1e-6
LOG2E = 1.4426950408889634

LANES = 128
MXU_DIM = 256
SEQ_TILE = 768
SSM_CHUNK = 16
VMEM_LIMIT = 52 * 1024 * 1024

COL_DQ, COL_DK, COL_DV, COL_SQ, COL_SK, COL_SV, COL_SU, COL_RQ, COL_RK, COL_RV, COL_RG = range(11)


def _padded_len(n_tok):
    lp = -(-n_tok // MXU_DIM) * MXU_DIM
    while lp % SEQ_TILE:
        lp += MXU_DIM
    return lp


def _params(*sem):
    return pltpu.CompilerParams(dimension_semantics=sem, vmem_limit_bytes=VMEM_LIMIT)


def _in_proj_kernel(x_ref, g_ref, w_ref, dc_ref, dsa_ref, dsb_ref, rc_ref, rs_ref, o_ref, su_ref, h_scr, su_scr,
                    *, tm, tiles_per_batch, pad_front):
    i = pl.program_id(0)
    j = pl.program_id(1)

    @pl.when(j == 0)
    def _():
        x = x_ref[...]
        y = x * lax.rsqrt(jnp.mean(x * x, axis=-1, keepdims=True) + EPS) * g_ref[...]
        pos = (i % tiles_per_batch) * tm + lax.broadcasted_iota(jnp.int32, (tm, 1), 0)
        h_scr[...] = jnp.where(pos >= pad_front, y, 0.0).astype(BF16)

    acc = jnp.dot(h_scr[...], w_ref[...], preferred_element_type=F32)

    is_diff = (j == COL_DQ) | (j == COL_DK)
    is_ret = (j == COL_RQ) | (j == COL_RK)

    @pl.when(is_diff)
    def _():
        scale = jnp.where(j == COL_DQ, LOG2E * DIFF_QK ** -0.5, 1.0).astype(F32)
        c, sa, sb = dc_ref[...], dsa_ref[...], dsb_ref[...]
        for n in range(GROUP_W // LANES):
            a = acc[:, n * LANES:(n + 1) * LANES]
            r = a * c + pltpu.roll(a, LANES - ROPE_DIM // 2, 1) * sa + pltpu.roll(a, ROPE_DIM // 2, 1) * sb
            o_ref[:, n * LANES:(n + 1) * LANES] = (r * scale).astype(BF16)

    @pl.when(is_ret)
    def _():
        scale = jnp.where(j == COL_RK, RET_DIM ** -0.5, 1.0).astype(F32)
        c, s = rc_ref[...], rs_ref[...]
        for n in range(GROUP_W // LANES):
            a = acc[:, n * LANES:(n + 1) * LANES]
            r = a * c + pltpu.roll(a, RET_DIM // 2, 1) * s
            o_ref[:, n * LANES:(n + 1) * LANES] = (r * scale).astype(BF16)

    @pl.when(jnp.logical_not(is_diff | is_ret))
    def _():
        scale = jnp.where(j == COL_SQ, LOG2E * SB_DIM ** -0.5, 1.0).astype(F32)
        o_ref[...] = (acc * scale).astype(BF16)

    @pl.when(j == COL_SU)
    def _():
        for n in range(GROUP_W // LANES):
            su_scr[n] = acc[:, n * LANES:(n + 1) * LANES]
            for s in range(SSM_CHUNK):
                piece = su_scr[n, pl.ds(s, tm // SSM_CHUNK, stride=SSM_CHUNK), :]
                su_ref[n, :, s * LANES:(s + 1) * LANES] = piece.astype(BF16)


def _in_proj(x2d, gain, w_bf16, tabs, *, lp, pad_front):
    rows = x2d.shape[0]
    tm = SEQ_TILE
    tiles_per_batch = lp // tm
    n_col = IN_COLS // GROUP_W
    tab_spec = pl.BlockSpec((tm, LANES), lambda i, j: (i % tiles_per_batch, 0))
    return pl.pallas_call(
        functools.partial(_in_proj_kernel, tm=tm, tiles_per_batch=tiles_per_batch, pad_front=pad_front),
        grid=(rows // tm, n_col),
        in_specs=[
            pl.BlockSpec((tm, D_MODEL), lambda i, j: (i, 0)),
            pl.BlockSpec((1, D_MODEL), lambda i, j: (0, 0)),
            pl.BlockSpec((D_MODEL, GROUP_W), lambda i, j: (0, j)),
            tab_spec, tab_spec, tab_spec, tab_spec, tab_spec,
        ],
        out_specs=[pl.BlockSpec((tm, GROUP_W), lambda i, j: (i, j)),
                   pl.BlockSpec((GROUP_W // LANES, tm // SSM_CHUNK, SSM_CHUNK * LANES), lambda i, j: (0, i, 0))],
        out_shape=[jax.ShapeDtypeStruct((rows, IN_COLS), BF16),
                   jax.ShapeDtypeStruct((GROUP_W // LANES, rows // SSM_CHUNK, SSM_CHUNK * LANES), BF16)],
        scratch_shapes=[pltpu.VMEM((tm, D_MODEL), BF16), pltpu.VMEM((GROUP_W // LANES, tm, LANES), F32)],
        compiler_params=_params("parallel", "arbitrary"),
        name="in_proj",
    )(x2d, gain.reshape(1, D_MODEL), w_bf16, *tabs)


def _rotary_tables(lp, pad_front):
    pos = (jnp.arange(lp, dtype=jnp.int32) - pad_front).astype(F32)[:, None]
    lane = jnp.arange(LANES)
    half = ROPE_DIM // 2
    inv = 1.0 / (ROPE_THETA ** (jnp.arange(half, dtype=F32) * (2.0 / ROPE_DIM)))
    ang = pos * inv[None, :]
    d = lane % DIFF_QK
    cos_l = jnp.cos(ang)[:, d % half]
    sin_l = jnp.sin(ang)[:, d % half]
    dc = jnp.where(d < ROPE_DIM, cos_l, 1.0)
    dsa = jnp.where(d < half, -sin_l, 0.0)
    dsb = jnp.where((d >= half) & (d < ROPE_DIM), sin_l, 0.0)
    rhalf = RET_DIM // 2
    rinv = 1.0 / (RET_THETA ** (jnp.arange(rhalf, dtype=F32) * (2.0 / RET_DIM)))
    rang = pos * rinv[None, :]
    r = lane % RET_DIM
    rc = jnp.cos(rang)[:, r % rhalf]
    rs = jnp.where(r < rhalf, -1.0, 1.0) * jnp.sin(rang)[:, r % rhalf]
    return dc, dsa, dsb, rc, rs


def _tile_pairs(nt, descending):
    pairs = [(i, j) for i in range(nt) for j in (range(i, -1, -1) if descending else range(i + 1))]
    return jnp.array(list(zip(*pairs)), dtype=jnp.int32)


def _dispatch_by_mask_kind(i, j, pad_front, step):
    has_pad = pad_front > 0

    @pl.when((j != i) & (j != 0))
    def _():
        step(causal=False, pad=False)

    @pl.when((j == i) & (i != 0))
    def _():
        step(causal=True, pad=False)

    @pl.when((j == 0) & (i != 0))
    def _():
        step(causal=False, pad=has_pad)

    @pl.when(i == 0)
    def _():
        step(causal=True, pad=has_pad)


def _diff_attn_kernel(ij_ref, lam_ref, q_ref, k_ref, v_ref, gn_ref, o_ref, m_scr, l_scr, acc_scr,
                      *, t, pad_front, lam_init):
    pair = pl.program_id(2)
    i = ij_ref[0, pair]
    j = ij_ref[1, pair]

    @pl.when(j == 0)
    def _():
        m_scr[...] = jnp.full(m_scr.shape, NEG_INF, F32)
        l_scr[...] = jnp.zeros(l_scr.shape, F32)
        acc_scr[...] = jnp.zeros(acc_scr.shape, F32)

    def step(causal, pad):
        q, k, v = q_ref[...], k_ref[...], v_ref[...]
        v_ext = jnp.concatenate([v, jnp.ones_like(v)], axis=1)
        mask = None
        if causal or pad:
            row = lax.broadcasted_iota(jnp.int32, (t, t), 0)
            col = lax.broadcasted_iota(jnp.int32, (t, t), 1)
            if causal:
                mask = col <= row
            if pad:
                mask = (col >= pad_front) if mask is None else mask & (col >= pad_front)
        lane = lax.broadcasted_iota(jnp.int32, q.shape, 1)
        scores = []
        for c in range(2):
            qc = jnp.where((lane >= c * DIFF_QK) & (lane < (c + 1) * DIFF_QK), q, jnp.zeros_like(q))
            s = lax.dot_general(qc, k, (((1,), (1,)), ((), ())), preferred_element_type=F32)
            if mask is not None:
                s = jnp.where(mask, s, NEG_INF)
            scores.append(s)
        for c, s in enumerate(scores):
            m_prev = m_scr[c]
            m_new = jnp.maximum(m_prev, jnp.max(s, axis=-1, keepdims=True))
            alpha = jnp.exp2(m_prev - m_new)
            p = jnp.exp2(s - jnp.tile(m_new, (1, t // LANES)))
            pv = jnp.dot(p.astype(BF16), v_ext, preferred_element_type=F32)
            acc_scr[c] = alpha * acc_scr[c] + pv[:, :DIFF_V]
            l_scr[c] = alpha * l_scr[c] + pv[:, DIFF_V:]
            m_scr[c] = m_new

    _dispatch_by_mask_kind(i, j, pad_front, step)

    @pl.when(j == i)
    def _():
        lv = lam_ref[...]
        lam = (jnp.exp(jnp.sum(lv[0:1] * lv[1:2], axis=-1, keepdims=True))
               - jnp.exp(jnp.sum(lv[2:3] * lv[3:4], axis=-1, keepdims=True)) + lam_init)
        o = acc_scr[0] / l_scr[0] - lam * (acc_scr[1] / l_scr[1])
        y = o * lax.rsqrt(jnp.mean(o * o, axis=-1, keepdims=True) + EPS) * gn_ref[...]
        o_ref[...] = (y * (1.0 - lam_init)).astype(BF16)


def _diff_attention(proj, lam_vecs, gn, *, bsz, lp, pad_front, lam_init):
    t = SEQ_TILE
    nt = lp // t
    ij = _tile_pairs(nt, descending=False)
    grid_spec = pltpu.PrefetchScalarGridSpec(
        num_scalar_prefetch=1,
        grid=(bsz, DIFF_HEADS, ij.shape[1]),
        in_specs=[
            pl.BlockSpec((4, DIFF_QK), lambda b, h, p, ij: (0, 0)),
            pl.BlockSpec((t, LANES), lambda b, h, p, ij: (b * nt + ij[0, p], COL_DQ * 4 + h)),
            pl.BlockSpec((t, LANES), lambda b, h, p, ij: (b * nt + ij[1, p], COL_DK * 4 + h)),
            pl.BlockSpec((t, LANES), lambda b, h, p, ij: (b * nt + ij[1, p], COL_DV * 4 + h)),
            pl.BlockSpec((1, DIFF_V), lambda b, h, p, ij: (0, 0)),
        ],
        out_specs=pl.BlockSpec((t, LANES), lambda b, h, p, ij: (b * nt + ij[0, p], h)),
        scratch_shapes=[pltpu.VMEM((2, t, LANES), F32), pltpu.VMEM((2, t, LANES), F32),
                        pltpu.VMEM((2, t, DIFF_V), F32)],
    )
    return pl.pallas_call(
        functools.partial(_diff_attn_kernel, t=t, pad_front=pad_front, lam_init=lam_init),
        grid_spec=grid_spec,
        out_shape=jax.ShapeDtypeStruct((bsz * lp, GROUP_W), BF16),
        compiler_params=_params("parallel", "parallel", "arbitrary"),
        name="diff_attn",
    )(ij, lam_vecs, proj, proj, proj, gn.reshape(1, DIFF_V))


def _sb_attn_kernel(ij_ref, q_ref, k_ref, v_ref, tri_ref, gn_ref, o_ref, run_scr, acc_scr,
                    *, t, sub, pad_front):
    pair = pl.program_id(2)
    i = ij_ref[0, pair]
    j = ij_ref[1, pair]

    @pl.when(j == i)
    def _():
        run_scr[...] = jnp.zeros(run_scr.shape, F32)
        acc_scr[...] = jnp.zeros(acc_scr.shape, F32)

    def block(qh, hh, r0, c0, mask):
        z = lax.dot_general(qh[r0:], k_ref[c0:c0 + sub, :], (((1,), (1,)), ((), ())),
                            preferred_element_type=F32)
        sign_bit = jnp.uint32(0x80000000)
        neg_abs = lax.bitcast_convert_type(lax.bitcast_convert_type(z, jnp.uint32) | sign_bit, F32)
        sp = jnp.maximum(z, 0.0) + jnp.log(1.0 + jnp.exp2(neg_abs)) * LOG2E
        if mask is not None:
            sp = jnp.where(mask, sp, 0.0)
        tail = jnp.dot(sp.astype(BF16), tri_ref[...], preferred_element_type=F32)
        run = run_scr[hh, r0:, :]
        log_w = z - tail - jnp.tile(run, (1, sub // LANES))
        if mask is not None:
            log_w = jnp.where(mask, log_w, NEG_INF)
        w = jnp.exp2(log_w)
        acc_scr[hh, r0:, :] += jnp.dot(w.astype(BF16), v_ref[c0:c0 + sub, :], preferred_element_type=F32)
        run_scr[hh, r0:, :] = run + jnp.sum(sp, axis=-1, keepdims=True)

    def step(causal, pad):
        q = q_ref[...]
        lane = lax.broadcasted_iota(jnp.int32, q.shape, 1)
        first_key = pad_front if pad else 0
        for hh in range(LANES // SB_DIM):
            qh = jnp.where((lane >= hh * SB_DIM) & (lane < (hh + 1) * SB_DIM), q, jnp.zeros_like(q))
            for c0 in reversed(range(0, t, sub)):
                if c0 + sub <= first_key:
                    continue
                r0 = c0 if causal else 0
                mask = None
                if causal or c0 < first_key:
                    row = r0 + lax.broadcasted_iota(jnp.int32, (t - r0, sub), 0)
                    col = c0 + lax.broadcasted_iota(jnp.int32, (t - r0, sub), 1)
                    if causal:
                        mask = col < row
                    if c0 < first_key:
                        mask = (col >= first_key) if mask is None else mask & (col >= first_key)
                block(qh, hh, r0, c0, mask)

    _dispatch_by_mask_kind(i, j, pad_front, step)

    @pl.when(j == 0)
    def _():
        lane = lax.broadcasted_iota(jnp.int32, (t, LANES), 1)
        y = jnp.zeros((t, LANES), F32)
        for hh in range(LANES // SB_DIM):
            in_head = (lane >= hh * SB_DIM) & (lane < (hh + 1) * SB_DIM)
            o = jnp.where(in_head, acc_scr[hh], 0.0)
            ms = jnp.sum(o * o, axis=-1, keepdims=True) * (1.0 / SB_DIM)
            y = y + o * lax.rsqrt(ms + EPS)
        o_ref[...] = (y * gn_ref[...]).astype(BF16)


def _sb_attention(proj, tri, gn, *, bsz, lp, pad_front):
    t = SEQ_TILE
    nt = lp // t
    n_pair = GROUP_W // LANES
    ij = _tile_pairs(nt, descending=True)
    grid_spec = pltpu.PrefetchScalarGridSpec(
        num_scalar_prefetch=1,
        grid=(bsz, n_pair, ij.shape[1]),
        in_specs=[
            pl.BlockSpec((t, LANES), lambda b, h, p, ij: (b * nt + ij[0, p], COL_SQ * 4 + h)),
            pl.BlockSpec((t, LANES), lambda b, h, p, ij: (b * nt + ij[1, p], COL_SK * 4 + h)),
            pl.BlockSpec((t, LANES), lambda b, h, p, ij: (b * nt + ij[1, p], COL_SV * 4 + h)),
            pl.BlockSpec((MXU_DIM, MXU_DIM), lambda b, h, p, ij: (0, 0)),
            pl.BlockSpec((1, LANES), lambda b, h, p, ij: (0, 0)),
        ],
        out_specs=pl.BlockSpec((t, LANES), lambda b, h, p, ij: (b * nt + ij[0, p], h)),
        scratch_shapes=[pltpu.VMEM((2, t, LANES), F32), pltpu.VMEM((2, t, LANES), F32)],
    )
    return pl.pallas_call(
        functools.partial(_sb_attn_kernel, t=t, sub=MXU_DIM, pad_front=pad_front),
        grid_spec=grid_spec,
        out_shape=jax.ShapeDtypeStruct((bsz * lp, GROUP_W), BF16),
        compiler_params=_params("parallel", "parallel", "arbitrary"),
        name="sb_attn",
    )(ij, proj, proj, proj, tri, jnp.tile(gn.reshape(1, SB_DIM), (1, LANES // SB_DIM)))


def _s5_matmul_kernel(u_ref, w_ref, o_ref):
    o_ref[0] = jnp.dot(u_ref[0], w_ref[0], preferred_element_type=F32)


def _s5_state_kernel(u_ref, w_ref, o_ref):
    o_ref[...] = jnp.dot(u_ref[0], w_ref[0], preferred_element_type=F32)


def _s5_scan_kernel(s_ref, a1_ref, a2_ref, x_ref, *, n_chunk, bsz, groups):
    a1 = a1_ref[...]
    a2 = a2_ref[...]
    w = groups * LANES

    def body(c, carry):
        new = []
        for b in range(bsz):
            x, xs = carry[b]
            row = s_ref[pl.ds(b * n_chunk + c, 1), :]
            x_ref[pl.ds(b * n_chunk + c, 1), :] = x
            s_x = jnp.concatenate([row[:, g * 2 * LANES:g * 2 * LANES + LANES] for g in range(groups)], axis=1)
            s_xs = jnp.concatenate([row[:, g * 2 * LANES + LANES:(g + 1) * 2 * LANES] for g in range(groups)], axis=1)
            new.append((a1 * x + a2 * xs + s_x, a1 * xs - a2 * x + s_xs))
        return tuple(new)

    zero = jnp.zeros((1, w), F32)
    lax.fori_loop(0, n_chunk, body, tuple((zero, zero) for _ in range(bsz)))


def _s5_cross_kernel(yi_ref, x_ref, cm_ref, y_ref):
    y_ref[0] = yi_ref[0] + jnp.dot(x_ref[...].astype(BF16), cm_ref[0], preferred_element_type=F32)


def _s5_out_kernel(y_ref, u_ref, d_ref, w_ref, b_ref, gn_ref, o_ref, y_scr):
    n_chunk = y_ref.shape[1]
    n_blk = GROUP_W // LANES
    for n in range(n_blk):
        for s in range(SSM_CHUNK):
            y_scr[n, pl.ds(s, n_chunk, stride=SSM_CHUNK), :] = y_ref[n, :, s * LANES:(s + 1) * LANES]
    y = jnp.concatenate([y_scr[n] for n in range(n_blk)], axis=1) + d_ref[...] * u_ref[...].astype(F32)
    g = 0.5 * y * (1.0 + jnp.tanh(math.sqrt(2.0 / math.pi) * (y + 0.044715 * (y * y * y))))
    gate = jnp.dot(g.astype(BF16), w_ref[...], preferred_element_type=F32) + b_ref[...]
    o = g * (1.0 / (1.0 + jnp.exp(-gate)))
    o_ref[...] = (o * lax.rsqrt(jnp.mean(o * o, axis=-1, keepdims=True) + EPS) * gn_ref[...]).astype(BF16)


def _s5_matrices(a_re, a_im, log_dt, b_re, b_im, c_re, c_im):
    hp = lax.Precision.HIGHEST
    t = SSM_CHUNK
    dt = jnp.exp(log_dt.astype(F32))[:, None]
    ar, ai = a_re.astype(F32), a_im.astype(F32)
    dpow = jnp.arange(t + 1, dtype=F32)[:, None, None]
    mag = jnp.exp(dpow * (ar * dt)[None])
    pr = mag * jnp.cos(dpow * (ai * dt)[None])
    pi = mag * jnp.sin(dpow * (ai * dt)[None])
    nr, ni = pr[1] - 1.0, pi[1]
    den = ar * ar + ai * ai
    fr = (nr * ar + ni * ai) / den
    fi = (ni * ar - nr * ai) / den
    br, bi = b_re.astype(F32), b_im.astype(F32)
    bbr = fr[..., None] * br - fi[..., None] * bi
    bbi = fr[..., None] * bi + fi[..., None] * br
    cr, ci = c_re.astype(F32), c_im.astype(F32)
    car = cr[None] * pr[:, :, None, :] - ci[None] * pi[:, :, None, :]
    cai = cr[None] * pi[:, :, None, :] + ci[None] * pr[:, :, None, :]
    kern = (jnp.einsum('dghp,gpk->dghk', car[:t], bbr, precision=hp)
            - jnp.einsum('dghp,gpk->dghk', cai[:t], bbi, precision=hp))
    gl = LANES // SSM_GROUP
    nb = SSM_NG // gl
    eye = jnp.eye(gl, dtype=F32)
    lag = jnp.arange(t)[None, :] - jnp.arange(t)[:, None]
    pick = (lag[:, :, None] == jnp.arange(t)[None, None, :]).astype(F32)
    m = jnp.einsum('std,dghk->stghk', pick, kern, precision=hp)
    m = m.reshape(t, t, nb, gl, SSM_GROUP, SSM_GROUP)
    w_intra = (jnp.transpose(m, (2, 0, 3, 5, 1, 4))[:, :, :, :, :, None, :]
               * eye[None, None, :, None, None, :, None]).reshape(nb, t * LANES, t * LANES)
    rpow = (t - 1.0) - jnp.arange(t, dtype=F32)[:, None, None]
    rmag = jnp.exp(rpow * (ar * dt)[None])
    rev_r = rmag * jnp.cos(rpow * (ai * dt)[None])
    rev_i = rmag * jnp.sin(rpow * (ai * dt)[None])
    sre = rev_r[..., None] * bbr[None] - rev_i[..., None] * bbi[None]
    sim = rev_r[..., None] * bbi[None] + rev_i[..., None] * bbr[None]
    st = jnp.concatenate([sre, sim, sim, sre], axis=2)
    st = st.reshape(t, nb, gl, 4 * SSM_STATE, SSM_GROUP)
    w_state = (jnp.transpose(st, (1, 0, 2, 4, 3))[:, :, :, :, None, :]
               * eye[None, None, :, None, :, None]).reshape(nb, t * LANES, gl * 4 * SSM_STATE)
    cx = jnp.concatenate([car[1:], -cai[1:]], axis=-1)
    cx = cx.reshape(t, nb, gl, SSM_GROUP, 2 * SSM_STATE)
    w_cross = (jnp.transpose(cx, (1, 2, 4, 0, 3))[:, :, :, :, None, :]
               * eye[None, :, None, None, :, None]).reshape(nb, gl * 2 * SSM_STATE, t * LANES)
    a1 = jnp.concatenate([pr[t], pr[t]], axis=-1).reshape(1, SSM_NG * LANES)
    a2 = jnp.concatenate([-pi[t], pi[t]], axis=-1).reshape(1, SSM_NG * LANES)
    return w_intra.astype(BF16), w_state.astype(BF16), w_cross.astype(BF16), a1, a2


def _s5_mixer(proj, u_c, mats, d, w_glu, b_glu, gn, *, bsz, lp):
    w_intra, w_state, w_cross, a1, a2 = mats
    t = SSM_CHUNK
    nb = GROUP_W // LANES
    gl = LANES // SSM_GROUP
    cl = t * LANES
    sl = gl * 4 * SSM_STATE
    xl = gl * 2 * SSM_STATE
    n_chunk = lp // t
    nc = bsz * n_chunk
    rows = bsz * lp
    rt = nc // 2

    y_intra = pl.pallas_call(
        _s5_matmul_kernel,
        grid=(nb, nc // rt),
        in_specs=[
            pl.BlockSpec((1, rt, cl), lambda n, r: (n, r, 0)),
            pl.BlockSpec((1, cl, cl), lambda n, r: (n, 0, 0)),
        ],
        out_specs=pl.BlockSpec((1, rt, cl), lambda n, r: (n, r, 0)),
        out_shape=jax.ShapeDtypeStruct((nb, nc, cl), F32),
        compiler_params=_params("parallel", "parallel"),
        name="s5_intra",
    )(u_c, w_intra)

    s_end = pl.pallas_call(
        _s5_state_kernel,
        grid=(nb, nc // rt),
        in_specs=[
            pl.BlockSpec((1, rt, cl), lambda n, r: (n, r, 0)),
            pl.BlockSpec((1, cl, sl), lambda n, r: (n, 0, 0)),
        ],
        out_specs=pl.BlockSpec((rt, sl), lambda n, r: (r, n)),
        out_shape=jax.ShapeDtypeStruct((nc, nb * sl), F32),
        compiler_params=_params("parallel", "parallel"),
        name="s5_state",
    )(u_c, w_state)

    x_start = pl.pallas_call(
        functools.partial(_s5_scan_kernel, n_chunk=n_chunk, bsz=bsz, groups=gl),
        grid=(nb,),
        in_specs=[
            pl.BlockSpec((nc, sl), lambda n: (0, n)),
            pl.BlockSpec((1, xl), lambda n: (0, n)),
            pl.BlockSpec((1, xl), lambda n: (0, n)),
        ],
        out_specs=pl.BlockSpec((nc, xl), lambda n: (0, n)),
        out_shape=jax.ShapeDtypeStruct((nc, nb * xl), F32),
        compiler_params=_params("parallel"),
        name="s5_scan",
    )(s_end, a1, a2)

    y_c = pl.pallas_call(
        _s5_cross_kernel,
        grid=(nb, nc // rt),
        in_specs=[
            pl.BlockSpec((1, rt, cl), lambda n, r: (n, r, 0)),
            pl.BlockSpec((rt, xl), lambda n, r: (r, n)),
            pl.BlockSpec((1, xl, cl), lambda n, r: (n, 0, 0)),
        ],
        out_specs=pl.BlockSpec((1, rt, cl), lambda n, r: (n, r, 0)),
        out_shape=jax.ShapeDtypeStruct((nb, nc, cl), F32),
        compiler_params=_params("parallel", "parallel"),
        name="s5_cross",
    )(y_intra, x_start, w_cross)

    tm = SEQ_TILE
    vec = lambda: pl.BlockSpec((1, GROUP_W), lambda i: (0, 0))
    return pl.pallas_call(
        _s5_out_kernel,
        grid=(rows // tm,),
        in_specs=[
            pl.BlockSpec((nb, tm // t, cl), lambda i: (0, i, 0)),
            pl.BlockSpec((tm, GROUP_W), lambda i: (i, COL_SU)),
            vec(),
            pl.BlockSpec((GROUP_W, GROUP_W), lambda i: (0, 0)),
            vec(), vec(),
        ],
        out_specs=pl.BlockSpec((tm, GROUP_W), lambda i: (i, 0)),
        out_shape=jax.ShapeDtypeStruct((rows, GROUP_W), BF16),
        scratch_shapes=[pltpu.VMEM((nb, tm, LANES), F32)],
        compiler_params=_params("parallel"),
        name="s5_out",
    )(y_c, proj, d.reshape(1, GROUP_W), w_glu.astype(BF16), b_glu.reshape(1, GROUP_W), gn.reshape(1, GROUP_W))


def _retention_kernel(lg_ref, q_ref, k_ref, v_ref, g_ref, gn_ref, o_ref, st_scr, *, t):
    c = pl.program_id(2)

    @pl.when(c == 0)
    def _():
        st_scr[...] = jnp.zeros(st_scr.shape, F32)

    lg = lg_ref[0][:, :1]
    q, k, v = q_ref[...], k_ref[...], v_ref[...]
    ti = lax.broadcasted_iota(jnp.int32, (t, t), 0)
    si = lax.broadcasted_iota(jnp.int32, (t, t), 1)
    rel = (ti - si).astype(F32)
    dmat = jnp.where(ti >= si, jnp.exp(lg * jnp.maximum(rel, 0.0)), 0.0)
    col = lax.broadcasted_iota(jnp.int32, (t, 1), 0).astype(F32)
    q_decay = jnp.exp(lg * (col + 1.0))
    k_decay = jnp.exp(lg * (t - 1.0 - col))
    chunk_decay = jnp.exp(lg * float(t))

    s = lax.dot_general(q, k, (((1,), (1,)), ((), ())), preferred_element_type=F32) * dmat
    state = st_scr[...]
    o = (jnp.dot(s.astype(BF16), v, preferred_element_type=F32)
         + jnp.dot(q, state.astype(BF16), preferred_element_type=F32) * q_decay)
    kd = (k.astype(F32) * k_decay).astype(BF16)
    st_scr[...] = chunk_decay * state + lax.dot_general(kd, v, (((0,), (0,)), ((), ())),
                                                        preferred_element_type=F32)

    mu = jnp.mean(o, axis=-1, keepdims=True)
    oc = o - mu
    y = oc * lax.rsqrt(jnp.mean(oc * oc, axis=-1, keepdims=True) + EPS) * gn_ref[...]
    gate = g_ref[...].astype(F32)
    o_ref[...] = (y * (gate / (1.0 + jnp.exp(-gate)))).astype(BF16)


def _retention(proj, gn, *, bsz, lp):
    t = SEQ_TILE
    nt = lp // t
    log_gamma = jnp.log1p(-jnp.exp2(-5.0 - jnp.arange(RET_HEADS, dtype=F32)))
    lg = jnp.broadcast_to(log_gamma[:, None, None], (RET_HEADS, 1, LANES))
    blk = lambda off: pl.BlockSpec((t, LANES), lambda b, h, c: (b * nt + c, off * 4 + h))
    return pl.pallas_call(
        functools.partial(_retention_kernel, t=t),
        grid=(bsz, RET_HEADS, nt),
        in_specs=[
            pl.BlockSpec((1, 1, LANES), lambda b, h, c: (h, 0, 0)),
            blk(COL_RQ), blk(COL_RK), blk(COL_RV), blk(COL_RG),
            pl.BlockSpec((1, RET_DIM), lambda b, h, c: (0, 0)),
        ],
        out_specs=pl.BlockSpec((t, LANES), lambda b, h, c: (b * nt + c, h)),
        out_shape=jax.ShapeDtypeStruct((bsz * lp, GROUP_W), BF16),
        scratch_shapes=[pltpu.VMEM((RET_DIM, RET_DIM), F32)],
        compiler_params=_params("parallel", "parallel", "arbitrary"),
        name="retention",
    )(lg, proj, proj, proj, proj, gn.reshape(1, RET_DIM))


def _out_proj_kernel(a_ref, b_ref, c_ref, d_ref, w_ref, x_ref, g_ref, o_ref):
    acc = jnp.dot(a_ref[...], w_ref[0], preferred_element_type=F32)
    acc += jnp.dot(b_ref[...], w_ref[1], preferred_element_type=F32)
    acc += jnp.dot(c_ref[...], w_ref[2], preferred_element_type=F32)
    acc += jnp.dot(d_ref[...], w_ref[3], preferred_element_type=F32)
    y = acc * lax.rsqrt(jnp.mean(acc * acc, axis=-1, keepdims=True) + EPS) * g_ref[...]
    o_ref[...] = x_ref[...] + y


def _out_proj(mixes, w_bf16, x2d, gain):
    rows = x2d.shape[0]
    tm = MXU_DIM
    mix_spec = pl.BlockSpec((tm, GROUP_W), lambda i: (i, 0))
    return pl.pallas_call(
        _out_proj_kernel,
        grid=(rows // tm,),
        in_specs=[
            mix_spec, mix_spec, mix_spec, mix_spec,
            pl.BlockSpec((4, GROUP_W, D_MODEL), lambda i: (0, 0, 0)),
            pl.BlockSpec((tm, D_MODEL), lambda i: (i, 0)),
            pl.BlockSpec((1, D_MODEL), lambda i: (0, 0)),
        ],
        out_specs=pl.BlockSpec((tm, D_MODEL), lambda i: (i, 0)),
        out_shape=jax.ShapeDtypeStruct((rows, D_MODEL), F32),
        compiler_params=_params("parallel"),
        name="out_proj",
    )(*mixes, w_bf16.reshape(4, GROUP_W, D_MODEL), x2d, gain.reshape(1, D_MODEL))


def _ffn_kernel(x_ref, gpre_ref, wg_ref, wu_ref, wd_ref, gpost_ref, o_ref, h_scr):
    k = pl.program_id(1)

    @pl.when(k == 0)
    def _():
        x = x_ref[...]
        h_scr[...] = (x * lax.rsqrt(jnp.mean(x * x, axis=-1, keepdims=True) + EPS) * gpre_ref[...]).astype(BF16)

    h = h_scr[...]
    gate = jnp.dot(h, wg_ref[...], preferred_element_type=F32)
    up = jnp.dot(h, wu_ref[...], preferred_element_type=F32)
    act = (gate / (1.0 + jnp.exp(-gate)) * up).astype(BF16)
    part = jnp.dot(act, wd_ref[...], preferred_element_type=F32)

    @pl.when(k == 0)
    def _():
        o_ref[...] = part

    @pl.when(k > 0)
    def _():
        o_ref[...] += part

    @pl.when(k == pl.num_programs(1) - 1)
    def _():
        f = o_ref[...]
        o_ref[...] = x_ref[...] + f * lax.rsqrt(jnp.mean(f * f, axis=-1, keepdims=True) + EPS) * gpost_ref[...]


def _ffn(x2d, gpre, wg, wu, wd, gpost):
    rows = x2d.shape[0]
    tm = 2 * MXU_DIM
    tf = GROUP_W
    return pl.pallas_call(
        _ffn_kernel,
        grid=(rows // tm, FFN_HIDDEN // tf),
        in_specs=[
            pl.BlockSpec((tm, D_MODEL), lambda i, k: (i, 0)),
            pl.BlockSpec((1, D_MODEL), lambda i, k: (0, 0)),
            pl.BlockSpec((D_MODEL, tf), lambda i, k: (0, k)),
            pl.BlockSpec((D_MODEL, tf), lambda i, k: (0, k)),
            pl.BlockSpec((tf, D_MODEL), lambda i, k: (k, 0)),
            pl.BlockSpec((1, D_MODEL), lambda i, k: (0, 0)),
        ],
        out_specs=pl.BlockSpec((tm, D_MODEL), lambda i, k: (i, 0)),
        out_shape=jax.ShapeDtypeStruct((rows, D_MODEL), F32),
        scratch_shapes=[pltpu.VMEM((tm, D_MODEL), BF16)],
        compiler_params=_params("parallel", "arbitrary"),
        name="ffn",
    )(x2d, gpre.reshape(1, D_MODEL), wg, wu, wd, gpost.reshape(1, D_MODEL))


def kernel(x, meta_tokens, norm_mix_pre, w_in, diff_lambda_q1, diff_lambda_k1, diff_lambda_q2, diff_lambda_k2, diff_norm, sb_norm, ssm_a_re, ssm_a_im, ssm_log_dt, ssm_b_re, ssm_b_im, ssm_c_re, ssm_c_im, ssm_d, ssm_w_glu, ssm_b_glu, ssm_norm, ret_norm, w_out, norm_mix_post, norm_ffn_pre, w_ffn_gate, w_ffn_up, w_ffn_down, norm_ffn_post):
    bsz, seq, d_model = x.shape
    assert d_model == D_MODEL and meta_tokens.shape == (N_META, D_MODEL)
    depth = w_in.shape[0]
    n_tok = N_META + seq
    lp = _padded_len(n_tok)
    pad_front = lp - n_tok

    meta = jnp.broadcast_to(meta_tokens[None].astype(x.dtype), (bsz, N_META, D_MODEL))
    h = jnp.concatenate([jnp.zeros((bsz, pad_front, D_MODEL), x.dtype), meta, x], axis=1)
    h = h.reshape(bsz * lp, D_MODEL)

    tabs = _rotary_tables(lp, pad_front)
    tri = (jnp.arange(MXU_DIM)[:, None] >= jnp.arange(MXU_DIM)[None, :]).astype(BF16)

    for l in range(depth):
        lam_init = 0.8 - 0.6 * math.exp(-0.3 * l)
        proj, su_chunks = _in_proj(h, norm_mix_pre[l], w_in[l].astype(BF16), tabs, lp=lp, pad_front=pad_front)
        lam_vecs = jnp.stack([diff_lambda_q1[l], diff_lambda_k1[l], diff_lambda_q2[l], diff_lambda_k2[l]]).astype(F32)
        o_diff = _diff_attention(proj, lam_vecs, diff_norm[l], bsz=bsz, lp=lp, pad_front=pad_front, lam_init=lam_init)
        o_sb = _sb_attention(proj, tri, sb_norm[l], bsz=bsz, lp=lp, pad_front=pad_front)
        mats = _s5_matrices(ssm_a_re[l], ssm_a_im[l], ssm_log_dt[l], ssm_b_re[l], ssm_b_im[l], ssm_c_re[l], ssm_c_im[l])
        o_ssm = _s5_mixer(proj, su_chunks, mats, ssm_d[l], ssm_w_glu[l], ssm_b_glu[l], ssm_norm[l], bsz=bsz, lp=lp)
        o_ret = _retention(proj, ret_norm[l], bsz=bsz, lp=lp)
        h = _out_proj((o_diff, o_sb, o_ssm, o_ret), w_out[l].astype(BF16), h, norm_mix_post[l])
        h = _ffn(h, norm_ffn_pre[l], w_ffn_gate[l].astype(BF16), w_ffn_up[l].astype(BF16),
                 w_ffn_down[l].astype(BF16), norm_ffn_post[l])

    return h.reshape(bsz, lp, D_MODEL)[:, pad_front + N_META:]
```

```python
import functools
import math

import jax
import jax.numpy as jnp
from jax import lax
from jax.experimental import pallas as pl
from jax.experimental.pallas import tpu as pltpu

F32 = jnp.float32
BF16 = jnp.bfloat16

D_MODEL = 2048
N_META = 16
GROUP_W = D_MODEL // 4
DIFF_HEADS = 4
DIFF_QK = 64
DIFF_V = GROUP_W // DIFF_HEADS
SB_HEADS = 8
SB_DIM = GROUP_W // SB_HEADS
SSM_GROUP = 16
SSM_NG = GROUP_W // SSM_GROUP
SSM_STATE = 64
RET_HEADS = 4
RET_DIM = GROUP_W // RET_HEADS
FFN_HIDDEN = -(-8 * D_MODEL // (3 * 256)) * 256
IN_COLS = 11 * GROUP_W
ROPE_THETA = 500000.0
ROPE_DIM = DIFF_QK // 4
RET_THETA = 10000.0
NEG_INF = -1e30
EPS = 1e-6
LOG2E = 1.4426950408889634

LANES = 128
MXU_DIM = 256
SEQ_TILE = 768
SSM_CHUNK = 16
VMEM_LIMIT = 52 * 1024 * 1024

COL_DQ, COL_DK, COL_DV, COL_SQ, COL_SK, COL_SV, COL_SU, COL_RQ, COL_RK, COL_RV, COL_RG = range(11)


def _padded_len(n_tok):
    lp = -(-n_tok // MXU_DIM) * MXU_DIM
    while lp % SEQ_TILE:
        lp += MXU_DIM
    return lp


def _params(*sem):
    return pltpu.CompilerParams(dimension_semantics=sem, vmem_limit_bytes=VMEM_LIMIT)


def _in_proj_kernel(x_ref, g_ref, w_ref, dc_ref, dsa_ref, dsb_ref, rc_ref, rs_ref, o_ref, su_ref, h_scr, su_scr,
                    *, tm, tiles_per_batch, pad_front):
    i = pl.program_id(0)
    j = pl.program_id(1)

    @pl.when(j == 0)
    def _():
        x = x_ref[...]
        y = x * lax.rsqrt(jnp.mean(x * x, axis=-1, keepdims=True) + EPS) * g_ref[...]
        pos = (i % tiles_per_batch) * tm + lax.broadcasted_iota(jnp.int32, (tm, 1), 0)
        h_scr[...] = jnp.where(pos >= pad_front, y, 0.0).astype(BF16)

    acc = jnp.dot(h_scr[...], w_ref[...], preferred_element_type=F32)

    is_diff = (j == COL_DQ) | (j == COL_DK)
    is_ret = (j == COL_RQ) | (j == COL_RK)

    @pl.when(is_diff)
    def _():
        scale = jnp.where(j == COL_DQ, LOG2E * DIFF_QK ** -0.5, 1.0).astype(F32)
        c, sa, sb = dc_ref[...], dsa_ref[...], dsb_ref[...]
        for n in range(GROUP_W // LANES):
            a = acc[:, n * LANES:(n + 1) * LANES]
            r = a * c + pltpu.roll(a, LANES - ROPE_DIM // 2, 1) * sa + pltpu.roll(a, ROPE_DIM // 2, 1) * sb
            o_ref[:, n * LANES:(n + 1) * LANES] = (r * scale).astype(BF16)

    @pl.when(is_ret)
    def _():
        scale = jnp.where(j == COL_RK, RET_DIM ** -0.5, 1.0).astype(F32)
        c, s = rc_ref[...], rs_ref[...]
        for n in range(GROUP_W // LANES):
            a = acc[:, n * LANES:(n + 1) * LANES]
            r = a * c + pltpu.roll(a, RET_DIM // 2, 1) * s
            o_ref[:, n * LANES:(n + 1) * LANES] = (r * scale).astype(BF16)

    @pl.when(jnp.logical_not(is_diff | is_ret))
    def _():
        scale = jnp.where(j == COL_SQ, LOG2E * SB_DIM ** -0.5, 1.0).astype(F32)
        o_ref[...] = (acc * scale).astype(BF16)

    @pl.when(j == COL_SU)
    def _():
        for n in range(GROUP_W // LANES):
            su_scr[n] = acc[:, n * LANES:(n + 1) * LANES]
            for s in range(SSM_CHUNK):
                piece = su_scr[n, pl.ds(s, tm // SSM_CHUNK, stride=SSM_CHUNK), :]
                su_ref[n, :, s * LANES:(s + 1) * LANES] = piece.astype(BF16)


def _in_proj(x2d, gain, w_bf16, tabs, *, lp, pad_front):
    rows = x2d.shape[0]
    tm = SEQ_TILE
    tiles_per_batch = lp // tm
    n_col = IN_COLS // GROUP_W
    tab_spec = pl.BlockSpec((tm, LANES), lambda i, j: (i % tiles_per_batch, 0))
    return pl.pallas_call(
        functools.partial(_in_proj_kernel, tm=tm, tiles_per_batch=tiles_per_batch, pad_front=pad_front),
        grid=(rows // tm, n_col),
        in_specs=[
            pl.BlockSpec((tm, D_MODEL), lambda i, j: (i, 0)),
            pl.BlockSpec((1, D_MODEL), lambda i, j: (0, 0)),
            pl.BlockSpec((D_MODEL, GROUP_W), lambda i, j: (0, j)),
            tab_spec, tab_spec, tab_spec, tab_spec, tab_spec,
        ],
        out_specs=[pl.BlockSpec((tm, GROUP_W), lambda i, j: (i, j)),
                   pl.BlockSpec((GROUP_W // LANES, tm // SSM_CHUNK, SSM_CHUNK * LANES), lambda i, j: (0, i, 0))],
        out_shape=[jax.ShapeDtypeStruct((rows, IN_COLS), BF16),
                   jax.ShapeDtypeStruct((GROUP_W // LANES, rows // SSM_CHUNK, SSM_CHUNK * LANES), BF16)],
        scratch_shapes=[pltpu.VMEM((tm, D_MODEL), BF16), pltpu.VMEM((GROUP_W // LANES, tm, LANES), F32)],
        compiler_params=_params("parallel", "arbitrary"),
        name="in_proj",
    )(x2d, gain.reshape(1, D_MODEL), w_bf16, *tabs)


def _rotary_tables(lp, pad_front):
    pos = (jnp.arange(lp, dtype=jnp.int32) - pad_front).astype(F32)[:, None]
    lane = jnp.arange(LANES)
    half = ROPE_DIM // 2
    inv = 1.0 / (ROPE_THETA ** (jnp.arange(half, dtype=F32) * (2.0 / ROPE_DIM)))
    ang = pos * inv[None, :]
    d = lane % DIFF_QK
    cos_l = jnp.cos(ang)[:, d % half]
    sin_l = jnp.sin(ang)[:, d % half]
    dc = jnp.where(d < ROPE_DIM, cos_l, 1.0)
    dsa = jnp.where(d < half, -sin_l, 0.0)
    dsb = jnp.where((d >= half) & (d < ROPE_DIM), sin_l, 0.0)
    rhalf = RET_DIM // 2
    rinv = 1.0 / (RET_THETA ** (jnp.arange(rhalf, dtype=F32) * (2.0 / RET_DIM)))
    rang = pos * rinv[None, :]
    r = lane % RET_DIM
    rc = jnp.cos(rang)[:, r % rhalf]
    rs = jnp.where(r < rhalf, -1.0, 1.0) * jnp.sin(rang)[:, r % rhalf]
    return dc, dsa, dsb, rc, rs


def _tile_pairs(nt, descending):
    pairs = [(i, j) for i in range(nt) for j in (range(i, -1, -1) if descending else range(i + 1))]
    return jnp.array(list(zip(*pairs)), dtype=jnp.int32)


def _dispatch_by_mask_kind(i, j, pad_front, step):
    has_pad = pad_front > 0

    @pl.when((j != i) & (j != 0))
    def _():
        step(causal=False, pad=False)

    @pl.when((j == i) & (i != 0))
    def _():
        step(causal=True, pad=False)

    @pl.when((j == 0) & (i != 0))
    def _():
        step(causal=False, pad=has_pad)

    @pl.when(i == 0)
    def _():
        step(causal=True, pad=has_pad)


def _diff_attn_kernel(ij_ref, lam_ref, q_ref, k_ref, v_ref, gn_ref, o_ref, m_scr, l_scr, acc_scr,
                      *, t, pad_front, lam_init):
    pair = pl.program_id(2)
    i = ij_ref[0, pair]
    j = ij_ref[1, pair]

    @pl.when(j == 0)
    def _():
        m_scr[...] = jnp.full(m_scr.shape, NEG_INF, F32)
        l_scr[...] = jnp.zeros(l_scr.shape, F32)
        acc_scr[...] = jnp.zeros(acc_scr.shape, F32)

    def step(causal, pad):
        q, k, v = q_ref[...], k_ref[...], v_ref[...]
        v_ext = jnp.concatenate([v, jnp.ones_like(v)], axis=1)
        mask = None
        if causal or pad:
            row = lax.broadcasted_iota(jnp.int32, (t, t), 0)
            col = lax.broadcasted_iota(jnp.int32, (t, t), 1)
            if causal:
                mask = col <= row
            if pad:
                mask = (col >= pad_front) if mask is None else mask & (col >= pad_front)
        lane = lax.broadcasted_iota(jnp.int32, q.shape, 1)
        scores = []
        for c in range(2):
            qc = jnp.where((lane >= c * DIFF_QK) & (lane < (c + 1) * DIFF_QK), q, jnp.zeros_like(q))
            s = lax.dot_general(qc, k, (((1,), (1,)), ((), ())), preferred_element_type=F32)
            if mask is not None:
                s = jnp.where(mask, s, NEG_INF)
            scores.append(s)
        for c, s in enumerate(scores):
            m_prev = m_scr[c]
            m_new = jnp.maximum(m_prev, jnp.max(s, axis=-1, keepdims=True))
            alpha = jnp.exp2(m_prev - m_new)
            p = jnp.exp2(s - jnp.tile(m_new, (1, t // LANES)))
            pv = jnp.dot(p.astype(BF16), v_ext, preferred_element_type=F32)
            acc_scr[c] = alpha * acc_scr[c] + pv[:, :DIFF_V]
            l_scr[c] = alpha * l_scr[c] + pv[:, DIFF_V:]
            m_scr[c] = m_new

    _dispatch_by_mask_kind(i, j, pad_front, step)

    @pl.when(j == i)
    def _():
        lv = lam_ref[...]
        lam = (jnp.exp(jnp.sum(lv[0:1] * lv[1:2], axis=-1, keepdims=True))
               - jnp.exp(jnp.sum(lv[2:3] * lv[3:4], axis=-1, keepdims=True)) + lam_init)
        o = acc_scr[0] / l_scr[0] - lam * (acc_scr[1] / l_scr[1])
        y = o * lax.rsqrt(jnp.mean(o * o, axis=-1, keepdims=True) + EPS) * gn_ref[...]
        o_ref[...] = (y * (1.0 - lam_init)).astype(BF16)


def _diff_attention(proj, lam_vecs, gn, *, bsz, lp, pad_front, lam_init):
    t = SEQ_TILE
    nt = lp // t
    ij = _tile_pairs(nt, descending=False)
    grid_spec = pltpu.PrefetchScalarGridSpec(
        num_scalar_prefetch=1,
        grid=(bsz, DIFF_HEADS, ij.shape[1]),
        in_specs=[
            pl.BlockSpec((4, DIFF_QK), lambda b, h, p, ij: (0, 0)),
            pl.BlockSpec((t, LANES), lambda b, h, p, ij: (b * nt + ij[0, p], COL_DQ * 4 + h)),
            pl.BlockSpec((t, LANES), lambda b, h, p, ij: (b * nt + ij[1, p], COL_DK * 4 + h)),
            pl.BlockSpec((t, LANES), lambda b, h, p, ij: (b * nt + ij[1, p], COL_DV * 4 + h)),
            pl.BlockSpec((1, DIFF_V), lambda b, h, p, ij: (0, 0)),
        ],
        out_specs=pl.BlockSpec((t, LANES), lambda b, h, p, ij: (b * nt + ij[0, p], h)),
        scratch_shapes=[pltpu.VMEM((2, t, LANES), F32), pltpu.VMEM((2, t, LANES), F32),
                        pltpu.VMEM((2, t, DIFF_V), F32)],
    )
    return pl.pallas_call(
        functools.partial(_diff_attn_kernel, t=t, pad_front=pad_front, lam_init=lam_init),
        grid_spec=grid_spec,
        out_shape=jax.ShapeDtypeStruct((bsz * lp, GROUP_W), BF16),
        compiler_params=_params("parallel", "parallel", "arbitrary"),
        name="diff_attn",
    )(ij, lam_vecs, proj, proj, proj, gn.reshape(1, DIFF_V))


def _sb_attn_kernel(ij_ref, q_ref, k_ref, v_ref, tri_ref, gn_ref, o_ref, run_scr, acc_scr,
                    *, t, sub, pad_front):
    pair = pl.program_id(2)
    i = ij_ref[0, pair]
    j = ij_ref[1, pair]

    @pl.when(j == i)
    def _():
        run_scr[...] = jnp.zeros(run_scr.shape, F32)
        acc_scr[...] = jnp.zeros(acc_scr.shape, F32)

    def block(qh, hh, r0, c0, mask):
        z = lax.dot_general(qh[r0:], k_ref[c0:c0 + sub, :], (((1,), (1,)), ((), ())),
                            preferred_element_type=F32)
        sign_bit = jnp.uint32(0x80000000)
        neg_abs = lax.bitcast_convert_type(lax.bitcast_convert_type(z, jnp.uint32) | sign_bit, F32)
        sp = jnp.maximum(z, 0.0) + jnp.log(1.0 + jnp.exp2(neg_abs)) * LOG2E
        if mask is not None:
            sp = jnp.where(mask, sp, 0.0)
        tail = jnp.dot(sp.astype(BF16), tri_ref[...], preferred_element_type=F32)
        run = run_scr[hh, r0:, :]
        log_w = z - tail - jnp.tile(run, (1, sub // LANES))
        if mask is not None:
            log_w = jnp.where(mask, log_w, NEG_INF)
        w = jnp.exp2(log_w)
        acc_scr[hh, r0:, :] += jnp.dot(w.astype(BF16), v_ref[c0:c0 + sub, :], preferred_element_type=F32)
        run_scr[hh, r0:, :] = run + jnp.sum(sp, axis=-1, keepdims=True)

    def step(causal, pad):
        q = q_ref[...]
        lane = lax.broadcasted_iota(jnp.int32, q.shape, 1)
        first_key = pad_front if pad else 0
        for hh in range(LANES // SB_DIM):
            qh = jnp.where((lane >= hh * SB_DIM) & (lane < (hh + 1) * SB_DIM), q, jnp.zeros_like(q))
            for c0 in reversed(range(0, t, sub)):
                if c0 + sub <= first_key:
                    continue
                r0 = c0 if causal else 0
                mask = None
                if causal or c0 < first_key:
                    row = r0 + lax.broadcasted_iota(jnp.int32, (t - r0, sub), 0)
                    col = c0 + lax.broadcasted_iota(jnp.int32, (t - r0, sub), 1)
                    if causal:
                        mask = col < row
                    if c0 < first_key:
                        mask = (col >= first_key) if mask is None else mask & (col >= first_key)
                block(qh, hh, r0, c0, mask)

    _dispatch_by_mask_kind(i, j, pad_front, step)

    @pl.when(j == 0)
    def _():
        lane = lax.broadcasted_iota(jnp.int32, (t, LANES), 1)
        y = jnp.zeros((t, LANES), F32)
        for hh in range(LANES // SB_DIM):
            in_head = (lane >= hh * SB_DIM) & (lane < (hh + 1) * SB_DIM)
            o = jnp.where(in_head, acc_scr[hh], 0.0)
            ms = jnp.sum(o * o, axis=-1, keepdims=True) * (1.0 / SB_DIM)
            y = y + o * lax.rsqrt(ms + EPS)
        o_ref[...] = (y * gn_ref[...]).astype(BF16)


def _sb_attention(proj, tri, gn, *, bsz, lp, pad_front):
    t = SEQ_TILE
    nt = lp // t
    n_pair = GROUP_W // LANES
    ij = _tile_pairs(nt, descending=True)
    grid_spec = pltpu.PrefetchScalarGridSpec(
        num_scalar_prefetch=1,
        grid=(bsz, n_pair, ij.shape[1]),
        in_specs=[
            pl.BlockSpec((t, LANES), lambda b, h, p, ij: (b * nt + ij[0, p], COL_SQ * 4 + h)),
            pl.BlockSpec((t, LANES), lambda b, h, p, ij: (b * nt + ij[1, p], COL_SK * 4 + h)),
            pl.BlockSpec((t, LANES), lambda b, h, p, ij: (b * nt + ij[1, p], COL_SV * 4 + h)),
            pl.BlockSpec((MXU_DIM, MXU_DIM), lambda b, h, p, ij: (0, 0)),
            pl.BlockSpec((1, LANES), lambda b, h, p, ij: (0, 0)),
        ],
        out_specs=pl.BlockSpec((t, LANES), lambda b, h, p, ij: (b * nt + ij[0, p], h)),
        scratch_shapes=[pltpu.VMEM((2, t, LANES), F32), pltpu.VMEM((2, t, LANES), F32)],
    )
    return pl.pallas_call(
        functools.partial(_sb_attn_kernel, t=t, sub=MXU_DIM, pad_front=pad_front),
        grid_spec=grid_spec,
        out_shape=jax.ShapeDtypeStruct((bsz * lp, GROUP_W), BF16),
        compiler_params=_params("parallel", "parallel", "arbitrary"),
        name="sb_attn",
    )(ij, proj, proj, proj, tri, jnp.tile(gn.reshape(1, SB_DIM), (1, LANES // SB_DIM)))


def _s5_intra_kernel(u_ref, d_ref, y_ref):
    zero = jnp.zeros((LANES, LANES), BF16)
    for t0 in range(0, SSM_CHUNK, 2):
        w = jnp.concatenate(
            [jnp.concatenate([d_ref[0, t0 - s] if s <= t0 else zero, d_ref[0, t0 + 1 - s]], axis=1)
             for s in range(t0 + 2)], axis=0)
        y_ref[0, :, t0 * LANES:(t0 + 2) * LANES] = jnp.dot(
            u_ref[0, :, :(t0 + 2) * LANES], w, preferred_element_type=F32)


def _s5_state_kernel(u_ref, e_ref, o_ref):
    u = u_ref[0]
    cols = 4 * SSM_STATE
    row = lax.broadcasted_iota(jnp.int32, (LANES, cols), 0)
    for g in range(LANES // SSM_GROUP):
        in_group = (row >= g * SSM_GROUP) & (row < (g + 1) * SSM_GROUP)
        w = jnp.concatenate([jnp.where(in_group, e_ref[0, s], jnp.zeros((LANES, cols), BF16))
                             for s in range(SSM_CHUNK)], axis=0)
        o_ref[:, g * cols:(g + 1) * cols] = jnp.dot(u, w, preferred_element_type=F32)


def _s5_scan_kernel(s_ref, a1_ref, a2_ref, x_ref, *, n_chunk, bsz, groups):
    a1 = a1_ref[...]
    a2 = a2_ref[...]
    w = groups * LANES

    def body(c, carry):
        new = []
        for b in range(bsz):
            x, xs = carry[b]
            row = s_ref[pl.ds(b * n_chunk + c, 1), :]
            x_ref[pl.ds(b * n_chunk + c, 1), :] = x
            s_x = jnp.concatenate([row[:, g * 2 * LANES:g * 2 * LANES + LANES] for g in range(groups)], axis=1)
            s_xs = jnp.concatenate([row[:, g * 2 * LANES + LANES:(g + 1) * 2 * LANES] for g in range(groups)], axis=1)
            new.append((a1 * x + a2 * xs + s_x, a1 * xs - a2 * x + s_xs))
        return tuple(new)

    zero = jnp.zeros((1, w), F32)
    lax.fori_loop(0, n_chunk, body, tuple((zero, zero) for _ in range(bsz)))


def _s5_cross_kernel(yi_ref, x_ref, c_ref, y_ref):
    x = x_ref[...].astype(BF16)
    for t0 in range(0, SSM_CHUNK, 2):
        w = jnp.concatenate([c_ref[0, t0], c_ref[0, t0 + 1]], axis=1)
        cols = slice(t0 * LANES, (t0 + 2) * LANES)
        y_ref[0, :, cols] = yi_ref[0, :, cols] + jnp.dot(x, w, preferred_element_type=F32)


def _s5_out_kernel(y_ref, u_ref, d_ref, w_ref, b_ref, gn_ref, o_ref, y_scr):
    n_chunk = y_ref.shape[1]
    n_blk = GROUP_W // LANES
    for n in range(n_blk):
        for s in range(SSM_CHUNK):
            y_scr[n, pl.ds(s, n_chunk, stride=SSM_CHUNK), :] = y_ref[n, :, s * LANES:(s + 1) * LANES]
    y = jnp.concatenate([y_scr[n] for n in range(n_blk)], axis=1) + d_ref[...] * u_ref[...].astype(F32)
    g = 0.5 * y * (1.0 + jnp.tanh(math.sqrt(2.0 / math.pi) * (y + 0.044715 * (y * y * y))))
    gate = jnp.dot(g.astype(BF16), w_ref[...], preferred_element_type=F32) + b_ref[...]
    o = g * (1.0 / (1.0 + jnp.exp(-gate)))
    o_ref[...] = (o * lax.rsqrt(jnp.mean(o * o, axis=-1, keepdims=True) + EPS) * gn_ref[...]).astype(BF16)


def _s5_matrices(a_re, a_im, log_dt, b_re, b_im, c_re, c_im):
    hp = lax.Precision.HIGHEST
    t = SSM_CHUNK
    dt = jnp.exp(log_dt.astype(F32))[:, None]
    ar, ai = a_re.astype(F32), a_im.astype(F32)
    dpow = jnp.arange(t + 1, dtype=F32)[:, None, None]
    mag = jnp.exp(dpow * (ar * dt)[None])
    pr = mag * jnp.cos(dpow * (ai * dt)[None])
    pi = mag * jnp.sin(dpow * (ai * dt)[None])
    nr, ni = pr[1] - 1.0, pi[1]
    den = ar * ar + ai * ai
    fr = (nr * ar + ni * ai) / den
    fi = (ni * ar - nr * ai) / den
    br, bi = b_re.astype(F32), b_im.astype(F32)
    bbr = fr[..., None] * br - fi[..., None] * bi
    bbi = fr[..., None] * bi + fi[..., None] * br
    cr, ci = c_re.astype(F32), c_im.astype(F32)
    car = cr[None] * pr[:, :, None, :] - ci[None] * pi[:, :, None, :]
    cai = cr[None] * pi[:, :, None, :] + ci[None] * pr[:, :, None, :]
    kern = (jnp.einsum('dghp,gpk->dghk', car[:t], bbr, precision=hp)
            - jnp.einsum('dghp,gpk->dghk', cai[:t], bbi, precision=hp))
    gl = LANES // SSM_GROUP
    nb = SSM_NG // gl
    place = (jnp.arange(LANES)[None, None, :]
             == jnp.arange(gl)[:, None, None] * SSM_GROUP + jnp.arange(SSM_GROUP)[None, :, None]).astype(F32)
    kern_b = kern.reshape(t, nb, gl, SSM_GROUP, SSM_GROUP)
    d_blocks = jnp.einsum('dbghk,ghl->bdgkl', kern_b, place, precision=hp).reshape(nb, t, LANES, LANES)
    rpow = (t - 1.0) - jnp.arange(t, dtype=F32)[:, None, None]
    rmag = jnp.exp(rpow * (ar * dt)[None])
    rev_r = (rmag * jnp.cos(rpow * (ai * dt)[None]))[:, :, None, :]
    rev_i = (rmag * jnp.sin(rpow * (ai * dt)[None]))[:, :, None, :]
    bbr_t, bbi_t = jnp.swapaxes(bbr, 1, 2)[None], jnp.swapaxes(bbi, 1, 2)[None]
    sre = rev_r * bbr_t - rev_i * bbi_t
    sim = rev_r * bbi_t + rev_i * bbr_t
    st = jnp.concatenate([sre, sim, sim, sre], axis=-1)
    e_blocks = jnp.transpose(st.reshape(t, nb, LANES, 4 * SSM_STATE), (1, 0, 2, 3))
    cx = jnp.concatenate([car[1:], -cai[1:]], axis=-1)
    cx = cx.reshape(t, nb, gl, SSM_GROUP, 2 * SSM_STATE)
    c_blocks = jnp.einsum('tbghc,ghl->btgcl', cx, place, precision=hp).reshape(nb, t, gl * 2 * SSM_STATE, LANES)
    a1 = jnp.concatenate([pr[t], pr[t]], axis=-1).reshape(1, SSM_NG * LANES)
    a2 = jnp.concatenate([-pi[t], pi[t]], axis=-1).reshape(1, SSM_NG * LANES)
    return d_blocks.astype(BF16), e_blocks.astype(BF16), c_blocks.astype(BF16), a1, a2


def _s5_mixer(proj, u_c, mats, d, w_glu, b_glu, gn, *, bsz, lp):
    d_blocks, e_blocks, c_blocks, a1, a2 = mats
    t = SSM_CHUNK
    nb = GROUP_W // LANES
    gl = LANES // SSM_GROUP
    cl = t * LANES
    sl = gl * 4 * SSM_STATE
    xl = gl * 2 * SSM_STATE
    n_chunk = lp // t
    nc = bsz * n_chunk
    rows = bsz * lp
    rt = nc // 2

    y_intra = pl.pallas_call(
        _s5_intra_kernel,
        grid=(nb, nc // rt),
        in_specs=[
            pl.BlockSpec((1, rt, cl), lambda n, r: (n, r, 0)),
            pl.BlockSpec((1, t, LANES, LANES), lambda n, r: (n, 0, 0, 0)),
        ],
        out_specs=pl.BlockSpec((1, rt, cl), lambda n, r: (n, r, 0)),
        out_shape=jax.ShapeDtypeStruct((nb, nc, cl), F32),
        compiler_params=_params("parallel", "parallel"),
        name="s5_intra",
    )(u_c, d_blocks)

    s_end = pl.pallas_call(
        _s5_state_kernel,
        grid=(nb, nc // rt),
        in_specs=[
            pl.BlockSpec((1, rt, cl), lambda n, r: (n, r, 0)),
            pl.BlockSpec((1, t, LANES, 4 * SSM_STATE), lambda n, r: (n, 0, 0, 0)),
        ],
        out_specs=pl.BlockSpec((rt, sl), lambda n, r: (r, n)),
        out_shape=jax.ShapeDtypeStruct((nc, nb * sl), F32),
        compiler_params=_params("parallel", "parallel"),
        name="s5_state",
    )(u_c, e_blocks)

    x_start = pl.pallas_call(
        functools.partial(_s5_scan_kernel, n_chunk=n_chunk, bsz=bsz, groups=gl),
        grid=(nb,),
        in_specs=[
            pl.BlockSpec((nc, sl), lambda n: (0, n)),
            pl.BlockSpec((1, xl), lambda n: (0, n)),
            pl.BlockSpec((1, xl), lambda n: (0, n)),
        ],
        out_specs=pl.BlockSpec((nc, xl), lambda n: (0, n)),
        out_shape=jax.ShapeDtypeStruct((nc, nb * xl), F32),
        compiler_params=_params("parallel"),
        name="s5_scan",
    )(s_end, a1, a2)

    y_c = pl.pallas_call(
        _s5_cross_kernel,
        grid=(nb, nc // rt),
        in_specs=[
            pl.BlockSpec((1, rt, cl), lambda n, r: (n, r, 0)),
            pl.BlockSpec((rt, xl), lambda n, r: (r, n)),
            pl.BlockSpec((1, t, xl, LANES), lambda n, r: (n, 0, 0, 0)),
        ],
        out_specs=pl.BlockSpec((1, rt, cl), lambda n, r: (n, r, 0)),
        out_shape=jax.ShapeDtypeStruct((nb, nc, cl), F32),
        compiler_params=_params("parallel", "parallel"),
        name="s5_cross",
    )(y_intra, x_start, c_blocks)

    tm = SEQ_TILE
    vec = lambda: pl.BlockSpec((1, GROUP_W), lambda i: (0, 0))
    return pl.pallas_call(
        _s5_out_kernel,
        grid=(rows // tm,),
        in_specs=[
            pl.BlockSpec((nb, tm // t, cl), lambda i: (0, i, 0)),
            pl.BlockSpec((tm, GROUP_W), lambda i: (i, COL_SU)),
            vec(),
            pl.BlockSpec((GROUP_W, GROUP_W), lambda i: (0, 0)),
            vec(), vec(),
        ],
        out_specs=pl.BlockSpec((tm, GROUP_W), lambda i: (i, 0)),
        out_shape=jax.ShapeDtypeStruct((rows, GROUP_W), BF16),
        scratch_shapes=[pltpu.VMEM((nb, tm, LANES), F32)],
        compiler_params=_params("parallel"),
        name="s5_out",
    )(y_c, proj, d.reshape(1, GROUP_W), w_glu.astype(BF16), b_glu.reshape(1, GROUP_W), gn.reshape(1, GROUP_W))


def _retention_kernel(lg_ref, q_ref, k_ref, v_ref, g_ref, gn_ref, o_ref, st_scr, *, t):
    c = pl.program_id(2)

    @pl.when(c == 0)
    def _():
        st_scr[...] = jnp.zeros(st_scr.shape, F32)

    lg = lg_ref[0][:, :1]
    q, k, v = q_ref[...], k_ref[...], v_ref[...]
    ti = lax.broadcasted_iota(jnp.int32, (t, t), 0)
    si = lax.broadcasted_iota(jnp.int32, (t, t), 1)
    rel = (ti - si).astype(F32)
    dmat = jnp.where(ti >= si, jnp.exp(lg * jnp.maximum(rel, 0.0)), 0.0)
    col = lax.broadcasted_iota(jnp.int32, (t, 1), 0).astype(F32)
    q_decay = jnp.exp(lg * (col + 1.0))
    k_decay = jnp.exp(lg * (t - 1.0 - col))
    chunk_decay = jnp.exp(lg * float(t))

    s = lax.dot_general(q, k, (((1,), (1,)), ((), ())), preferred_element_type=F32) * dmat
    state = st_scr[...]
    o = (jnp.dot(s.astype(BF16), v, preferred_element_type=F32)
         + jnp.dot(q, state.astype(BF16), preferred_element_type=F32) * q_decay)
    kd = (k.astype(F32) * k_decay).astype(BF16)
    st_scr[...] = chunk_decay * state + lax.dot_general(kd, v, (((0,), (0,)), ((), ())),
                                                        preferred_element_type=F32)

    mu = jnp.mean(o, axis=-1, keepdims=True)
    oc = o - mu
    y = oc * lax.rsqrt(jnp.mean(oc * oc, axis=-1, keepdims=True) + EPS) * gn_ref[...]
    gate = g_ref[...].astype(F32)
    o_ref[...] = (y * (gate / (1.0 + jnp.exp(-gate)))).astype(BF16)


def _retention(proj, gn, *, bsz, lp):
    t = SEQ_TILE
    nt = lp // t
    log_gamma = jnp.log1p(-jnp.exp2(-5.0 - jnp.arange(RET_HEADS, dtype=F32)))
    lg = jnp.broadcast_to(log_gamma[:, None, None], (RET_HEADS, 1, LANES))
    blk = lambda off: pl.BlockSpec((t, LANES), lambda b, h, c: (b * nt + c, off * 4 + h))
    return pl.pallas_call(
        functools.partial(_retention_kernel, t=t),
        grid=(bsz, RET_HEADS, nt),
        in_specs=[
            pl.BlockSpec((1, 1, LANES), lambda b, h, c: (h, 0, 0)),
            blk(COL_RQ), blk(COL_RK), blk(COL_RV), blk(COL_RG),
            pl.BlockSpec((1, RET_DIM), lambda b, h, c: (0, 0)),
        ],
        out_specs=pl.BlockSpec((t, LANES), lambda b, h, c: (b * nt + c, h)),
        out_shape=jax.ShapeDtypeStruct((bsz * lp, GROUP_W), BF16),
        scratch_shapes=[pltpu.VMEM((RET_DIM, RET_DIM), F32)],
        compiler_params=_params("parallel", "parallel", "arbitrary"),
        name="retention",
    )(lg, proj, proj, proj, proj, gn.reshape(1, RET_DIM))


def _out_proj_kernel(a_ref, b_ref, c_ref, d_ref, w_ref, x_ref, g_ref, o_ref):
    acc = jnp.dot(a_ref[...], w_ref[0], preferred_element_type=F32)
    acc += jnp.dot(b_ref[...], w_ref[1], preferred_element_type=F32)
    acc += jnp.dot(c_ref[...], w_ref[2], preferred_element_type=F32)
    acc += jnp.dot(d_ref[...], w_ref[3], preferred_element_type=F32)
    y = acc * lax.rsqrt(jnp.mean(acc * acc, axis=-1, keepdims=True) + EPS) * g_ref[...]
    o_ref[...] = x_ref[...] + y


def _out_proj(mixes, w_bf16, x2d, gain):
    rows = x2d.shape[0]
    tm = MXU_DIM
    mix_spec = pl.BlockSpec((tm, GROUP_W), lambda i: (i, 0))
    return pl.pallas_call(
        _out_proj_kernel,
        grid=(rows // tm,),
        in_specs=[
            mix_spec, mix_spec, mix_spec, mix_spec,
            pl.BlockSpec((4, GROUP_W, D_MODEL), lambda i: (0, 0, 0)),
            pl.BlockSpec((tm, D_MODEL), lambda i: (i, 0)),
            pl.BlockSpec((1, D_MODEL), lambda i: (0, 0)),
        ],
        out_specs=pl.BlockSpec((tm, D_MODEL), lambda i: (i, 0)),
        out_shape=jax.ShapeDtypeStruct((rows, D_MODEL), F32),
        compiler_params=_params("parallel"),
        name="out_proj",
    )(*mixes, w_bf16.reshape(4, GROUP_W, D_MODEL), x2d, gain.reshape(1, D_MODEL))


def _ffn_kernel(x_ref, gpre_ref, wg_ref, wu_ref, wd_ref, gpost_ref, o_ref, h_scr):
    k = pl.program_id(1)

    @pl.when(k == 0)
    def _():
        x = x_ref[...]
        h_scr[...] = (x * lax.rsqrt(jnp.mean(x * x, axis=-1, keepdims=True) + EPS) * gpre_ref[...]).astype(BF16)

    h = h_scr[...]
    gate = jnp.dot(h, wg_ref[...], preferred_element_type=F32)
    up = jnp.dot(h, wu_ref[...], preferred_element_type=F32)
    act = (gate / (1.0 + jnp.exp(-gate)) * up).astype(BF16)
    part = jnp.dot(act, wd_ref[...], preferred_element_type=F32)

    @pl.when(k == 0)
    def _():
        o_ref[...] = part

    @pl.when(k > 0)
    def _():
        o_ref[...] += part

    @pl.when(k == pl.num_programs(1) - 1)
    def _():
        f = o_ref[...]
        o_ref[...] = x_ref[...] + f * lax.rsqrt(jnp.mean(f * f, axis=-1, keepdims=True) + EPS) * gpost_ref[...]


def _ffn(x2d, gpre, wg, wu, wd, gpost):
    rows = x2d.shape[0]
    tm = 2 * MXU_DIM
    tf = GROUP_W
    return pl.pallas_call(
        _ffn_kernel,
        grid=(rows // tm, FFN_HIDDEN // tf),
        in_specs=[
            pl.BlockSpec((tm, D_MODEL), lambda i, k: (i, 0)),
            pl.BlockSpec((1, D_MODEL), lambda i, k: (0, 0)),
            pl.BlockSpec((D_MODEL, tf), lambda i, k: (0, k)),
            pl.BlockSpec((D_MODEL, tf), lambda i, k: (0, k)),
            pl.BlockSpec((tf, D_MODEL), lambda i, k: (k, 0)),
            pl.BlockSpec((1, D_MODEL), lambda i, k: (0, 0)),
        ],
        out_specs=pl.BlockSpec((tm, D_MODEL), lambda i, k: (i, 0)),
        out_shape=jax.ShapeDtypeStruct((rows, D_MODEL), F32),
        scratch_shapes=[pltpu.VMEM((tm, D_MODEL), BF16)],
        compiler_params=_params("parallel", "arbitrary"),
        name="ffn",
    )(x2d, gpre.reshape(1, D_MODEL), wg, wu, wd, gpost.reshape(1, D_MODEL))


def kernel(x, meta_tokens, norm_mix_pre, w_in, diff_lambda_q1, diff_lambda_k1, diff_lambda_q2, diff_lambda_k2, diff_norm, sb_norm, ssm_a_re, ssm_a_im, ssm_log_dt, ssm_b_re, ssm_b_im, ssm_c_re, ssm_c_im, ssm_d, ssm_w_glu, ssm_b_glu, ssm_norm, ret_norm, w_out, norm_mix_post, norm_ffn_pre, w_ffn_gate, w_ffn_up, w_ffn_down, norm_ffn_post):
    bsz, seq, d_model = x.shape
    assert d_model == D_MODEL and meta_tokens.shape == (N_META, D_MODEL)
    depth = w_in.shape[0]
    n_tok = N_META + seq
    lp = _padded_len(n_tok)
    pad_front = lp - n_tok

    meta = jnp.broadcast_to(meta_tokens[None].astype(x.dtype), (bsz, N_META, D_MODEL))
    h = jnp.concatenate([jnp.zeros((bsz, pad_front, D_MODEL), x.dtype), meta, x], axis=1)
    h = h.reshape(bsz * lp, D_MODEL)

    tabs = _rotary_tables(lp, pad_front)
    tri = (jnp.arange(MXU_DIM)[:, None] >= jnp.arange(MXU_DIM)[None, :]).astype(BF16)

    for l in range(depth):
        lam_init = 0.8 - 0.6 * math.exp(-0.3 * l)
        proj, su_chunks = _in_proj(h, norm_mix_pre[l], w_in[l].astype(BF16), tabs, lp=lp, pad_front=pad_front)
        lam_vecs = jnp.stack([diff_lambda_q1[l], diff_lambda_k1[l], diff_lambda_q2[l], diff_lambda_k2[l]]).astype(F32)
        o_diff = _diff_attention(proj, lam_vecs, diff_norm[l], bsz=bsz, lp=lp, pad_front=pad_front, lam_init=lam_init)
        o_sb = _sb_attention(proj, tri, sb_norm[l], bsz=bsz, lp=lp, pad_front=pad_front)
        mats = _s5_matrices(ssm_a_re[l], ssm_a_im[l], ssm_log_dt[l], ssm_b_re[l], ssm_b_im[l], ssm_c_re[l], ssm_c_im[l])
        o_ssm = _s5_mixer(proj, su_chunks, mats, ssm_d[l], ssm_w_glu[l], ssm_b_glu[l], ssm_norm[l], bsz=bsz, lp=lp)
        o_ret = _retention(proj, ret_norm[l], bsz=bsz, lp=lp)
        h = _out_proj((o_diff, o_sb, o_ssm, o_ret), w_out[l].astype(BF16), h, norm_mix_post[l])
        h = _ffn(h, norm_ffn_pre[l], w_ffn_gate[l].astype(BF16), w_ffn_up[l].astype(BF16),
                 w_ffn_down[l].astype(BF16), norm_ffn_post[l])

    return h.reshape(bsz, lp, D_MODEL)[:, pad_front + N_META:]
```

```python
import functools
import math

import jax
import jax.numpy as jnp
from jax import lax
from jax.experimental import pallas as pl
from jax.experimental.pallas import tpu as pltpu

F32 = jnp.float32
BF16 = jnp.bfloat16

D_MODEL = 2048
N_META = 16
GROUP_W = D_MODEL // 4
DIFF_HEADS = 4
DIFF_QK = 64
DIFF_V = GROUP_W // DIFF_HEADS
SB_HEADS = 8
SB_DIM = GROUP_W // SB_HEADS
SSM_GROUP = 16
SSM_NG = GROUP_W // SSM_GROUP
SSM_STATE = 64
RET_HEADS = 4
RET_DIM = GROUP_W // RET_HEADS
FFN_HIDDEN = -(-8 * D_MODEL // (3 * 256)) * 256
IN_COLS = 11 * GROUP_W
ROPE_THETA = 500000.0
ROPE_DIM = DIFF_QK // 4
RET_THETA = 10000.0
NEG_INF = -1e30
EPS = 1e-6
LOG2E = 1.4426950408889634

LANES = 128
MXU_DIM = 256
SEQ_TILE = 768
SSM_CHUNK = 16
DIFF_HEADS_PER_STEP = 2
SB_BLOCKS_PER_STEP = 2
VMEM_LIMIT = 52 * 1024 * 1024

COL_DQ, COL_DK, COL_DV, COL_SQ, COL_SK, COL_SV, COL_SU, COL_RQ, COL_RK, COL_RV, COL_RG = range(11)


def _padded_len(n_tok):
    lp = -(-n_tok // MXU_DIM) * MXU_DIM
    while lp % SEQ_TILE:
        lp += MXU_DIM
    return lp


def _params(*sem):
    return pltpu.CompilerParams(dimension_semantics=sem, vmem_limit_bytes=VMEM_LIMIT)


def _in_proj_kernel(x_ref, g_ref, w_ref, dc_ref, dsa_ref, dsb_ref, rc_ref, rs_ref, o_ref, su_ref, h_scr, su_scr,
                    *, tm, tiles_per_batch, pad_front):
    i = pl.program_id(0)
    j = pl.program_id(1)

    @pl.when(j == 0)
    def _():
        x = x_ref[...]
        y = x * lax.rsqrt(jnp.mean(x * x, axis=-1, keepdims=True) + EPS) * g_ref[...]
        pos = (i % tiles_per_batch) * tm + lax.broadcasted_iota(jnp.int32, (tm, 1), 0)
        h_scr[...] = jnp.where(pos >= pad_front, y, 0.0).astype(BF16)

    acc = jnp.dot(h_scr[...], w_ref[...], preferred_element_type=F32)

    is_diff = (j == COL_DQ) | (j == COL_DK)
    is_ret = (j == COL_RQ) | (j == COL_RK)

    @pl.when(is_diff)
    def _():
        scale = jnp.where(j == COL_DQ, LOG2E * DIFF_QK ** -0.5, 1.0).astype(F32)
        c, sa, sb = dc_ref[...], dsa_ref[...], dsb_ref[...]
        for n in range(GROUP_W // LANES):
            a = acc[:, n * LANES:(n + 1) * LANES]
            r = a * c + pltpu.roll(a, LANES - ROPE_DIM // 2, 1) * sa + pltpu.roll(a, ROPE_DIM // 2, 1) * sb
            o_ref[:, n * LANES:(n + 1) * LANES] = (r * scale).astype(BF16)

    @pl.when(is_ret)
    def _():
        scale = jnp.where(j == COL_RK, RET_DIM ** -0.5, 1.0).astype(F32)
        c, s = rc_ref[...], rs_ref[...]
        for n in range(GROUP_W // LANES):
            a = acc[:, n * LANES:(n + 1) * LANES]
            r = a * c + pltpu.roll(a, RET_DIM // 2, 1) * s
            o_ref[:, n * LANES:(n + 1) * LANES] = (r * scale).astype(BF16)

    @pl.when(jnp.logical_not(is_diff | is_ret))
    def _():
        scale = jnp.where(j == COL_SQ, LOG2E * SB_DIM ** -0.5, 1.0).astype(F32)
        o_ref[...] = (acc * scale).astype(BF16)

    @pl.when(j == COL_SU)
    def _():
        for n in range(GROUP_W // LANES):
            su_scr[n] = acc[:, n * LANES:(n + 1) * LANES]
            for s in range(SSM_CHUNK):
                piece = su_scr[n, pl.ds(s, tm // SSM_CHUNK, stride=SSM_CHUNK), :]
                su_ref[n, :, s * LANES:(s + 1) * LANES] = piece.astype(BF16)


def _in_proj(x2d, gain, w_bf16, tabs, *, lp, pad_front):
    rows = x2d.shape[0]
    tm = SEQ_TILE
    tiles_per_batch = lp // tm
    n_col = IN_COLS // GROUP_W
    tab_spec = pl.BlockSpec((tm, LANES), lambda i, j: (i % tiles_per_batch, 0))
    return pl.pallas_call(
        functools.partial(_in_proj_kernel, tm=tm, tiles_per_batch=tiles_per_batch, pad_front=pad_front),
        grid=(rows // tm, n_col),
        in_specs=[
            pl.BlockSpec((tm, D_MODEL), lambda i, j: (i, 0)),
            pl.BlockSpec((1, D_MODEL), lambda i, j: (0, 0)),
            pl.BlockSpec((D_MODEL, GROUP_W), lambda i, j: (0, j)),
            tab_spec, tab_spec, tab_spec, tab_spec, tab_spec,
        ],
        out_specs=[pl.BlockSpec((tm, GROUP_W), lambda i, j: (i, j)),
                   pl.BlockSpec((GROUP_W // LANES, tm // SSM_CHUNK, SSM_CHUNK * LANES), lambda i, j: (0, i, 0))],
        out_shape=[jax.ShapeDtypeStruct((rows, IN_COLS), BF16),
                   jax.ShapeDtypeStruct((GROUP_W // LANES, rows // SSM_CHUNK, SSM_CHUNK * LANES), BF16)],
        scratch_shapes=[pltpu.VMEM((tm, D_MODEL), BF16), pltpu.VMEM((GROUP_W // LANES, tm, LANES), F32)],
        compiler_params=_params("parallel", "arbitrary"),
        name="in_proj",
    )(x2d, gain.reshape(1, D_MODEL), w_bf16, *tabs)


def _rotary_tables(lp, pad_front):
    pos = (jnp.arange(lp, dtype=jnp.int32) - pad_front).astype(F32)[:, None]
    lane = jnp.arange(LANES)
    half = ROPE_DIM // 2
    inv = 1.0 / (ROPE_THETA ** (jnp.arange(half, dtype=F32) * (2.0 / ROPE_DIM)))
    ang = pos * inv[None, :]
    d = lane % DIFF_QK
    cos_l = jnp.cos(ang)[:, d % half]
    sin_l = jnp.sin(ang)[:, d % half]
    dc = jnp.where(d < ROPE_DIM, cos_l, 1.0)
    dsa = jnp.where(d < half, -sin_l, 0.0)
    dsb = jnp.where((d >= half) & (d < ROPE_DIM), sin_l, 0.0)
    rhalf = RET_DIM // 2
    rinv = 1.0 / (RET_THETA ** (jnp.arange(rhalf, dtype=F32) * (2.0 / RET_DIM)))
    rang = pos * rinv[None, :]
    r = lane % RET_DIM
    rc = jnp.cos(rang)[:, r % rhalf]
    rs = jnp.where(r < rhalf, -1.0, 1.0) * jnp.sin(rang)[:, r % rhalf]
    return dc, dsa, dsb, rc, rs


def _tile_pairs(nt, descending):
    pairs = [(i, j) for i in range(nt) for j in (range(i, -1, -1) if descending else range(i + 1))]
    return jnp.array(list(zip(*pairs)), dtype=jnp.int32)


def _dispatch_by_mask_kind(i, j, pad_front, step):
    has_pad = pad_front > 0

    @pl.when((j != i) & (j != 0))
    def _():
        step(causal=False, pad=False)

    @pl.when((j == i) & (i != 0))
    def _():
        step(causal=True, pad=False)

    @pl.when((j == 0) & (i != 0))
    def _():
        step(causal=False, pad=has_pad)

    @pl.when(i == 0)
    def _():
        step(causal=True, pad=has_pad)


def _diff_attn_kernel(ij_ref, lam_ref, q_ref, k_ref, v_ref, gn_ref, o_ref, m_scr, l_scr, acc_scr,
                      *, t, pad_front, lam_init):
    pair = pl.program_id(2)
    i = ij_ref[0, pair]
    j = ij_ref[1, pair]

    @pl.when(j == 0)
    def _():
        m_scr[...] = jnp.full(m_scr.shape, NEG_INF, F32)
        l_scr[...] = jnp.zeros(l_scr.shape, F32)
        acc_scr[...] = jnp.zeros(acc_scr.shape, F32)

    def step(causal, pad):
        mask = None
        if causal or pad:
            row = lax.broadcasted_iota(jnp.int32, (t, t), 0)
            col = lax.broadcasted_iota(jnp.int32, (t, t), 1)
            if causal:
                mask = col <= row
            if pad:
                mask = (col >= pad_front) if mask is None else mask & (col >= pad_front)
        lane = lax.broadcasted_iota(jnp.int32, (t, LANES), 1)
        scores, values = [], []
        for hd in range(DIFF_HEADS_PER_STEP):
            lanes = slice(hd * LANES, (hd + 1) * LANES)
            q, k, v = q_ref[:, lanes], k_ref[:, lanes], v_ref[:, lanes]
            values.append(jnp.concatenate([v, jnp.ones_like(v)], axis=1))
            for c in range(2):
                qc = jnp.where((lane >= c * DIFF_QK) & (lane < (c + 1) * DIFF_QK), q, jnp.zeros_like(q))
                s = lax.dot_general(qc, k, (((1,), (1,)), ((), ())), preferred_element_type=F32)
                if mask is not None:
                    s = jnp.where(mask, s, NEG_INF)
                scores.append(s)
        for n, s in enumerate(scores):
            m_prev = m_scr[n]
            m_new = jnp.maximum(m_prev, jnp.max(s, axis=-1, keepdims=True))
            alpha = jnp.exp2(m_prev - m_new)
            p = jnp.exp2(s - jnp.tile(m_new, (1, t // LANES)))
            pv = jnp.dot(p.astype(BF16), values[n // 2], preferred_element_type=F32)
            acc_scr[n] = alpha * acc_scr[n] + pv[:, :DIFF_V]
            l_scr[n] = alpha * l_scr[n] + pv[:, DIFF_V:]
            m_scr[n] = m_new

    _dispatch_by_mask_kind(i, j, pad_front, step)

    @pl.when(j == i)
    def _():
        lv = lam_ref[...]
        lam = (jnp.exp(jnp.sum(lv[0:1] * lv[1:2], axis=-1, keepdims=True))
               - jnp.exp(jnp.sum(lv[2:3] * lv[3:4], axis=-1, keepdims=True)) + lam_init)
        for hd in range(DIFF_HEADS_PER_STEP):
            a, b = 2 * hd, 2 * hd + 1
            o = acc_scr[a] / l_scr[a] - lam * (acc_scr[b] / l_scr[b])
            y = o * lax.rsqrt(jnp.mean(o * o, axis=-1, keepdims=True) + EPS) * gn_ref[...]
            o_ref[:, hd * LANES:(hd + 1) * LANES] = (y * (1.0 - lam_init)).astype(BF16)


def _diff_attention(proj, lam_vecs, gn, *, bsz, lp, pad_front, lam_init):
    t = SEQ_TILE
    nt = lp // t
    ij = _tile_pairs(nt, descending=False)
    hs = DIFF_HEADS_PER_STEP
    w = hs * LANES
    grid_spec = pltpu.PrefetchScalarGridSpec(
        num_scalar_prefetch=1,
        grid=(bsz, DIFF_HEADS // hs, ij.shape[1]),
        in_specs=[
            pl.BlockSpec((4, DIFF_QK), lambda b, h, p, ij: (0, 0)),
            pl.BlockSpec((t, w), lambda b, h, p, ij: (b * nt + ij[0, p], COL_DQ * GROUP_W // w + h)),
            pl.BlockSpec((t, w), lambda b, h, p, ij: (b * nt + ij[1, p], COL_DK * GROUP_W // w + h)),
            pl.BlockSpec((t, w), lambda b, h, p, ij: (b * nt + ij[1, p], COL_DV * GROUP_W // w + h)),
            pl.BlockSpec((1, DIFF_V), lambda b, h, p, ij: (0, 0)),
        ],
        out_specs=pl.BlockSpec((t, w), lambda b, h, p, ij: (b * nt + ij[0, p], h)),
        scratch_shapes=[pltpu.VMEM((2 * hs, t, LANES), F32), pltpu.VMEM((2 * hs, t, LANES), F32),
                        pltpu.VMEM((2 * hs, t, DIFF_V), F32)],
    )
    return pl.pallas_call(
        functools.partial(_diff_attn_kernel, t=t, pad_front=pad_front, lam_init=lam_init),
        grid_spec=grid_spec,
        out_shape=jax.ShapeDtypeStruct((bsz * lp, GROUP_W), BF16),
        compiler_params=_params("parallel", "parallel", "arbitrary"),
        name="diff_attn",
    )(ij, lam_vecs, proj, proj, proj, gn.reshape(1, DIFF_V))


def _sb_attn_kernel(ij_ref, q_ref, k_ref, v_ref, tri_ref, gn_ref, o_ref, run_scr, acc_scr,
                    *, t, sub, pad_front):
    pair = pl.program_id(2)
    i = ij_ref[0, pair]
    j = ij_ref[1, pair]

    @pl.when(j == i)
    def _():
        run_scr[...] = jnp.zeros(run_scr.shape, F32)
        acc_scr[...] = jnp.zeros(acc_scr.shape, F32)

    def block(qh, hh, lanes, r0, c0, mask):
        z = lax.dot_general(qh[r0:], k_ref[c0:c0 + sub, lanes], (((1,), (1,)), ((), ())),
                            preferred_element_type=F32)
        sign_bit = jnp.uint32(0x80000000)
        neg_abs = lax.bitcast_convert_type(lax.bitcast_convert_type(z, jnp.uint32) | sign_bit, F32)
        sp = jnp.maximum(z, 0.0) + jnp.log(1.0 + jnp.exp2(neg_abs)) * LOG2E
        if mask is not None:
            sp = jnp.where(mask, sp, 0.0)
        tail = jnp.dot(sp.astype(BF16), tri_ref[...], preferred_element_type=F32)
        run = run_scr[hh, r0:, :]
        log_w = z - tail - jnp.tile(run, (1, sub // LANES))
        if mask is not None:
            log_w = jnp.where(mask, log_w, NEG_INF)
        w = jnp.exp2(log_w)
        acc_scr[hh, r0:, :] += jnp.dot(w.astype(BF16), v_ref[c0:c0 + sub, lanes], preferred_element_type=F32)
        run_scr[hh, r0:, :] = run + jnp.sum(sp, axis=-1, keepdims=True)

    heads_per_block = LANES // SB_DIM

    def step(causal, pad):
        lane = lax.broadcasted_iota(jnp.int32, (t, LANES), 1)
        first_key = pad_front if pad else 0
        for blk in range(SB_BLOCKS_PER_STEP):
            lanes = slice(blk * LANES, (blk + 1) * LANES)
            q = q_ref[:, lanes]
            for hh in range(heads_per_block):
                qh = jnp.where((lane >= hh * SB_DIM) & (lane < (hh + 1) * SB_DIM), q, jnp.zeros_like(q))
                for c0 in reversed(range(0, t, sub)):
                    if c0 + sub <= first_key:
                        continue
                    r0 = c0 if causal else 0
                    mask = None
                    if causal or c0 < first_key:
                        row = r0 + lax.broadcasted_iota(jnp.int32, (t - r0, sub), 0)
                        col = c0 + lax.broadcasted_iota(jnp.int32, (t - r0, sub), 1)
                        if causal:
                            mask = col < row
                        if c0 < first_key:
                            mask = (col >= first_key) if mask is None else mask & (col >= first_key)
                    block(qh, blk * heads_per_block + hh, lanes, r0, c0, mask)

    _dispatch_by_mask_kind(i, j, pad_front, step)

    @pl.when(j == 0)
    def _():
        lane = lax.broadcasted_iota(jnp.int32, (t, LANES), 1)
        for blk in range(SB_BLOCKS_PER_STEP):
            y = jnp.zeros((t, LANES), F32)
            for hh in range(heads_per_block):
                in_head = (lane >= hh * SB_DIM) & (lane < (hh + 1) * SB_DIM)
                o = jnp.where(in_head, acc_scr[blk * heads_per_block + hh], 0.0)
                ms = jnp.sum(o * o, axis=-1, keepdims=True) * (1.0 / SB_DIM)
                y = y + o * lax.rsqrt(ms + EPS)
            o_ref[:, blk * LANES:(blk + 1) * LANES] = (y * gn_ref[...]).astype(BF16)


def _sb_attention(proj, tri, gn, *, bsz, lp, pad_front):
    t = SEQ_TILE
    nt = lp // t
    w = SB_BLOCKS_PER_STEP * LANES
    n_slot = SB_BLOCKS_PER_STEP * (LANES // SB_DIM)
    ij = _tile_pairs(nt, descending=True)
    grid_spec = pltpu.PrefetchScalarGridSpec(
        num_scalar_prefetch=1,
        grid=(bsz, GROUP_W // w, ij.shape[1]),
        in_specs=[
            pl.BlockSpec((t, w), lambda b, h, p, ij: (b * nt + ij[0, p], COL_SQ * GROUP_W // w + h)),
            pl.BlockSpec((t, w), lambda b, h, p, ij: (b * nt + ij[1, p], COL_SK * GROUP_W // w + h)),
            pl.BlockSpec((t, w), lambda b, h, p, ij: (b * nt + ij[1, p], COL_SV * GROUP_W // w + h)),
            pl.BlockSpec((MXU_DIM, MXU_DIM), lambda b, h, p, ij: (0, 0)),
            pl.BlockSpec((1, LANES), lambda b, h, p, ij: (0, 0)),
        ],
        out_specs=pl.BlockSpec((t, w), lambda b, h, p, ij: (b * nt + ij[0, p], h)),
        scratch_shapes=[pltpu.VMEM((n_slot, t, LANES), F32), pltpu.VMEM((n_slot, t, LANES), F32)],
    )
    return pl.pallas_call(
        functools.partial(_sb_attn_kernel, t=t, sub=MXU_DIM, pad_front=pad_front),
        grid_spec=grid_spec,
        out_shape=jax.ShapeDtypeStruct((bsz * lp, GROUP_W), BF16),
        compiler_params=_params("parallel", "parallel", "arbitrary"),
        name="sb_attn",
    )(ij, proj, proj, proj, tri, jnp.tile(gn.reshape(1, SB_DIM), (1, LANES // SB_DIM)))


def _s5_intra_kernel(u_ref, d_ref, y_ref):
    zero = jnp.zeros((LANES, LANES), BF16)
    for t0 in range(0, SSM_CHUNK, 2):
        w = jnp.concatenate(
            [jnp.concatenate([d_ref[0, t0 - s] if s <= t0 else zero, d_ref[0, t0 + 1 - s]], axis=1)
             for s in range(t0 + 2)], axis=0)
        y_ref[0, :, t0 * LANES:(t0 + 2) * LANES] = jnp.dot(
            u_ref[0, :, :(t0 + 2) * LANES], w, preferred_element_type=F32)


def _s5_state_kernel(u_ref, e_ref, o_ref):
    u = u_ref[0]
    cols = 4 * SSM_STATE
    row = lax.broadcasted_iota(jnp.int32, (LANES, cols), 0)
    for g in range(LANES // SSM_GROUP):
        in_group = (row >= g * SSM_GROUP) & (row < (g + 1) * SSM_GROUP)
        w = jnp.concatenate([jnp.where(in_group, e_ref[0, s], jnp.zeros((LANES, cols), BF16))
                             for s in range(SSM_CHUNK)], axis=0)
        o_ref[:, g * cols:(g + 1) * cols] = jnp.dot(u, w, preferred_element_type=F32)


def _s5_scan_kernel(s_ref, a1_ref, a2_ref, x_ref, *, n_chunk, bsz, groups):
    a1 = a1_ref[...]
    a2 = a2_ref[...]
    w = groups * LANES

    def body(c, carry):
        new = []
        for b in range(bsz):
            x, xs = carry[b]
            row = s_ref[pl.ds(b * n_chunk + c, 1), :]
            x_ref[pl.ds(b * n_chunk + c, 1), :] = x
            s_x = jnp.concatenate([row[:, g * 2 * LANES:g * 2 * LANES + LANES] for g in range(groups)], axis=1)
            s_xs = jnp.concatenate([row[:, g * 2 * LANES + LANES:(g + 1) * 2 * LANES] for g in range(groups)], axis=1)
            new.append((a1 * x + a2 * xs + s_x, a1 * xs - a2 * x + s_xs))
        return tuple(new)

    zero = jnp.zeros((1, w), F32)
    lax.fori_loop(0, n_chunk, body, tuple((zero, zero) for _ in range(bsz)))


def _s5_cross_kernel(yi_ref, x_ref, c_ref, y_ref):
    x = x_ref[...].astype(BF16)
    for t0 in range(0, SSM_CHUNK, 2):
        w = jnp.concatenate([c_ref[0, t0], c_ref[0, t0 + 1]], axis=1)
        cols = slice(t0 * LANES, (t0 + 2) * LANES)
        y_ref[0, :, cols] = yi_ref[0, :, cols] + jnp.dot(x, w, preferred_element_type=F32)


def _s5_out_kernel(y_ref, u_ref, d_ref, w_ref, b_ref, gn_ref, o_ref, y_scr):
    n_chunk = y_ref.shape[1]
    n_blk = GROUP_W // LANES
    for n in range(n_blk):
        for s in range(SSM_CHUNK):
            y_scr[n, pl.ds(s, n_chunk, stride=SSM_CHUNK), :] = y_ref[n, :, s * LANES:(s + 1) * LANES]
    y = jnp.concatenate([y_scr[n] for n in range(n_blk)], axis=1) + d_ref[...] * u_ref[...].astype(F32)
    g = 0.5 * y * (1.0 + jnp.tanh(math.sqrt(2.0 / math.pi) * (y + 0.044715 * (y * y * y))))
    gate = jnp.dot(g.astype(BF16), w_ref[...], preferred_element_type=F32) + b_ref[...]
    o = g * (1.0 / (1.0 + jnp.exp(-gate)))
    o_ref[...] = (o * lax.rsqrt(jnp.mean(o * o, axis=-1, keepdims=True) + EPS) * gn_ref[...]).astype(BF16)


def _s5_matrices(a_re, a_im, log_dt, b_re, b_im, c_re, c_im):
    hp = lax.Precision.HIGHEST
    t = SSM_CHUNK
    dt = jnp.exp(log_dt.astype(F32))[:, None]
    ar, ai = a_re.astype(F32), a_im.astype(F32)
    dpow = jnp.arange(t + 1, dtype=F32)[:, None, None]
    mag = jnp.exp(dpow * (ar * dt)[None])
    pr = mag * jnp.cos(dpow * (ai * dt)[None])
    pi = mag * jnp.sin(dpow * (ai * dt)[None])
    nr, ni = pr[1] - 1.0, pi[1]
    den = ar * ar + ai * ai
    fr = (nr * ar + ni * ai) / den
    fi = (ni * ar - nr * ai) / den
    br, bi = b_re.astype(F32), b_im.astype(F32)
    bbr = fr[..., None] * br - fi[..., None] * bi
    bbi = fr[..., None] * bi + fi[..., None] * br
    cr, ci = c_re.astype(F32), c_im.astype(F32)
    car = cr[None] * pr[:, :, None, :] - ci[None] * pi[:, :, None, :]
    cai = cr[None] * pi[:, :, None, :] + ci[None] * pr[:, :, None, :]
    kern = (jnp.einsum('dghp,gpk->dghk', car[:t], bbr, precision=hp)
            - jnp.einsum('dghp,gpk->dghk', cai[:t], bbi, precision=hp))
    gl = LANES // SSM_GROUP
    nb = SSM_NG // gl
    place = (jnp.arange(LANES)[None, None, :]
             == jnp.arange(gl)[:, None, None] * SSM_GROUP + jnp.arange(SSM_GROUP)[None, :, None]).astype(F32)
    kern_b = kern.reshape(t, nb, gl, SSM_GROUP, SSM_GROUP)
    d_blocks = jnp.einsum('dbghk,ghl->bdgkl', kern_b, place, precision=hp).reshape(nb, t, LANES, LANES)
    rpow = (t - 1.0) - jnp.arange(t, dtype=F32)[:, None, None]
    rmag = jnp.exp(rpow * (ar * dt)[None])
    rev_r = (rmag * jnp.cos(rpow * (ai * dt)[None]))[:, :, None, :]
    rev_i = (rmag * jnp.sin(rpow * (ai * dt)[None]))[:, :, None, :]
    bbr_t, bbi_t = jnp.swapaxes(bbr, 1, 2)[None], jnp.swapaxes(bbi, 1, 2)[None]
    sre = rev_r * bbr_t - rev_i * bbi_t
    sim = rev_r * bbi_t + rev_i * bbr_t
    st = jnp.concatenate([sre, sim, sim, sre], axis=-1)
    e_blocks = jnp.transpose(st.reshape(t, nb, LANES, 4 * SSM_STATE), (1, 0, 2, 3))
    cx = jnp.concatenate([car[1:], -cai[1:]], axis=-1)
    cx = cx.reshape(t, nb, gl, SSM_GROUP, 2 * SSM_STATE)
    c_blocks = jnp.einsum('tbghc,ghl->btgcl', cx, place, precision=hp).reshape(nb, t, gl * 2 * SSM_STATE, LANES)
    a1 = jnp.concatenate([pr[t], pr[t]], axis=-1).reshape(1, SSM_NG * LANES)
    a2 = jnp.concatenate([-pi[t], pi[t]], axis=-1).reshape(1, SSM_NG * LANES)
    return d_blocks.astype(BF16), e_blocks.astype(BF16), c_blocks.astype(BF16), a1, a2


def _s5_mixer(proj, u_c, mats, d, w_glu, b_glu, gn, *, bsz, lp):
    d_blocks, e_blocks, c_blocks, a1, a2 = mats
    t = SSM_CHUNK
    nb = GROUP_W // LANES
    gl = LANES // SSM_GROUP
    cl = t * LANES
    sl = gl * 4 * SSM_STATE
    xl = gl * 2 * SSM_STATE
    n_chunk = lp // t
    nc = bsz * n_chunk
    rows = bsz * lp
    rt = nc // 2

    y_intra = pl.pallas_call(
        _s5_intra_kernel,
        grid=(nb, nc // rt),
        in_specs=[
            pl.BlockSpec((1, rt, cl), lambda n, r: (n, r, 0)),
            pl.BlockSpec((1, t, LANES, LANES), lambda n, r: (n, 0, 0, 0)),
        ],
        out_specs=pl.BlockSpec((1, rt, cl), lambda n, r: (n, r, 0)),
        out_shape=jax.ShapeDtypeStruct((nb, nc, cl), F32),
        compiler_params=_params("parallel", "parallel"),
        name="s5_intra",
    )(u_c, d_blocks)

    s_end = pl.pallas_call(
        _s5_state_kernel,
        grid=(nb, nc // rt),
        in_specs=[
            pl.BlockSpec((1, rt, cl), lambda n, r: (n, r, 0)),
            pl.BlockSpec((1, t, LANES, 4 * SSM_STATE), lambda n, r: (n, 0, 0, 0)),
        ],
        out_specs=pl.BlockSpec((rt, sl), lambda n, r: (r, n)),
        out_shape=jax.ShapeDtypeStruct((nc, nb * sl), F32),
        compiler_params=_params("parallel", "parallel"),
        name="s5_state",
    )(u_c, e_blocks)

    x_start = pl.pallas_call(
        functools.partial(_s5_scan_kernel, n_chunk=n_chunk, bsz=bsz, groups=gl),
        grid=(nb,),
        in_specs=[
            pl.BlockSpec((nc, sl), lambda n: (0, n)),
            pl.BlockSpec((1, xl), lambda n: (0, n)),
            pl.BlockSpec((1, xl), lambda n: (0, n)),
        ],
        out_specs=pl.BlockSpec((nc, xl), lambda n: (0, n)),
        out_shape=jax.ShapeDtypeStruct((nc, nb * xl), F32),
        compiler_params=_params("parallel"),
        name="s5_scan",
    )(s_end, a1, a2)

    y_c = pl.pallas_call(
        _s5_cross_kernel,
        grid=(nb, nc // rt),
        in_specs=[
            pl.BlockSpec((1, rt, cl), lambda n, r: (n, r, 0)),
            pl.BlockSpec((rt, xl), lambda n, r: (r, n)),
            pl.BlockSpec((1, t, xl, LANES), lambda n, r: (n, 0, 0, 0)),
        ],
        out_specs=pl.BlockSpec((1, rt, cl), lambda n, r: (n, r, 0)),
        out_shape=jax.ShapeDtypeStruct((nb, nc, cl), F32),
        compiler_params=_params("parallel", "parallel"),
        name="s5_cross",
    )(y_intra, x_start, c_blocks)

    tm = SEQ_TILE
    vec = lambda: pl.BlockSpec((1, GROUP_W), lambda i: (0, 0))
    return pl.pallas_call(
        _s5_out_kernel,
        grid=(rows // tm,),
        in_specs=[
            pl.BlockSpec((nb, tm // t, cl), lambda i: (0, i, 0)),
            pl.BlockSpec((tm, GROUP_W), lambda i: (i, COL_SU)),
            vec(),
            pl.BlockSpec((GROUP_W, GROUP_W), lambda i: (0, 0)),
            vec(), vec(),
        ],
        out_specs=pl.BlockSpec((tm, GROUP_W), lambda i: (i, 0)),
        out_shape=jax.ShapeDtypeStruct((rows, GROUP_W), BF16),
        scratch_shapes=[pltpu.VMEM((nb, tm, LANES), F32)],
        compiler_params=_params("parallel"),
        name="s5_out",
    )(y_c, proj, d.reshape(1, GROUP_W), w_glu.astype(BF16), b_glu.reshape(1, GROUP_W), gn.reshape(1, GROUP_W))


def _retention_kernel(lg_ref, q_ref, k_ref, v_ref, g_ref, gn_ref, o_ref, st_scr, *, t):
    c = pl.program_id(2)

    @pl.when(c == 0)
    def _():
        st_scr[...] = jnp.zeros(st_scr.shape, F32)

    lg = lg_ref[0][:, :1]
    q, k, v = q_ref[...], k_ref[...], v_ref[...]
    ti = lax.broadcasted_iota(jnp.int32, (t, t), 0)
    si = lax.broadcasted_iota(jnp.int32, (t, t), 1)
    rel = (ti - si).astype(F32)
    dmat = jnp.where(ti >= si, jnp.exp(lg * jnp.maximum(rel, 0.0)), 0.0)
    col = lax.broadcasted_iota(jnp.int32, (t, 1), 0).astype(F32)
    q_decay = jnp.exp(lg * (col + 1.0))
    k_decay = jnp.exp(lg * (t - 1.0 - col))
    chunk_decay = jnp.exp(lg * float(t))

    s = lax.dot_general(q, k, (((1,), (1,)), ((), ())), preferred_element_type=F32) * dmat
    state = st_scr[...]
    o = (jnp.dot(s.astype(BF16), v, preferred_element_type=F32)
         + jnp.dot(q, state.astype(BF16), preferred_element_type=F32) * q_decay)
    kd = (k.astype(F32) * k_decay).astype(BF16)
    st_scr[...] = chunk_decay * state + lax.dot_general(kd, v, (((0,), (0,)), ((), ())),
                                                        preferred_element_type=F32)

    mu = jnp.mean(o, axis=-1, keepdims=True)
    oc = o - mu
    y = oc * lax.rsqrt(jnp.mean(oc * oc, axis=-1, keepdims=True) + EPS) * gn_ref[...]
    gate = g_ref[...].astype(F32)
    o_ref[...] = (y * (gate / (1.0 + jnp.exp(-gate)))).astype(BF16)


def _retention(proj, gn, *, bsz, lp):
    t = SEQ_TILE
    nt = lp // t
    log_gamma = jnp.log1p(-jnp.exp2(-5.0 - jnp.arange(RET_HEADS, dtype=F32)))
    lg = jnp.broadcast_to(log_gamma[:, None, None], (RET_HEADS, 1, LANES))
    blk = lambda off: pl.BlockSpec((t, LANES), lambda b, h, c: (b * nt + c, off * 4 + h))
    return pl.pallas_call(
        functools.partial(_retention_kernel, t=t),
        grid=(bsz, RET_HEADS, nt),
        in_specs=[
            pl.BlockSpec((1, 1, LANES), lambda b, h, c: (h, 0, 0)),
            blk(COL_RQ), blk(COL_RK), blk(COL_RV), blk(COL_RG),
            pl.BlockSpec((1, RET_DIM), lambda b, h, c: (0, 0)),
        ],
        out_specs=pl.BlockSpec((t, LANES), lambda b, h, c: (b * nt + c, h)),
        out_shape=jax.ShapeDtypeStruct((bsz * lp, GROUP_W), BF16),
        scratch_shapes=[pltpu.VMEM((RET_DIM, RET_DIM), F32)],
        compiler_params=_params("parallel", "parallel", "arbitrary"),
        name="retention",
    )(lg, proj, proj, proj, proj, gn.reshape(1, RET_DIM))


def _out_proj_kernel(a_ref, b_ref, c_ref, d_ref, w_ref, x_ref, g_ref, o_ref):
    acc = jnp.dot(a_ref[...], w_ref[0], preferred_element_type=F32)
    acc += jnp.dot(b_ref[...], w_ref[1], preferred_element_type=F32)
    acc += jnp.dot(c_ref[...], w_ref[2], preferred_element_type=F32)
    acc += jnp.dot(d_ref[...], w_ref[3], preferred_element_type=F32)
    y = acc * lax.rsqrt(jnp.mean(acc * acc, axis=-1, keepdims=True) + EPS) * g_ref[...]
    o_ref[...] = x_ref[...] + y


def _out_proj(mixes, w_bf16, x2d, gain):
    rows = x2d.shape[0]
    tm = MXU_DIM
    mix_spec = pl.BlockSpec((tm, GROUP_W), lambda i: (i, 0))
    return pl.pallas_call(
        _out_proj_kernel,
        grid=(rows // tm,),
        in_specs=[
            mix_spec, mix_spec, mix_spec, mix_spec,
            pl.BlockSpec((4, GROUP_W, D_MODEL), lambda i: (0, 0, 0)),
            pl.BlockSpec((tm, D_MODEL), lambda i: (i, 0)),
            pl.BlockSpec((1, D_MODEL), lambda i: (0, 0)),
        ],
        out_specs=pl.BlockSpec((tm, D_MODEL), lambda i: (i, 0)),
        out_shape=jax.ShapeDtypeStruct((rows, D_MODEL), F32),
        compiler_params=_params("parallel"),
        name="out_proj",
    )(*mixes, w_bf16.reshape(4, GROUP_W, D_MODEL), x2d, gain.reshape(1, D_MODEL))


def _ffn_kernel(x_ref, gpre_ref, wg_ref, wu_ref, wd_ref, gpost_ref, o_ref, h_scr):
    k = pl.program_id(1)

    @pl.when(k == 0)
    def _():
        x = x_ref[...]
        h_scr[...] = (x * lax.rsqrt(jnp.mean(x * x, axis=-1, keepdims=True) + EPS) * gpre_ref[...]).astype(BF16)

    h = h_scr[...]
    gate = jnp.dot(h, wg_ref[...], preferred_element_type=F32)
    up = jnp.dot(h, wu_ref[...], preferred_element_type=F32)
    act = (gate / (1.0 + jnp.exp(-gate)) * up).astype(BF16)
    part = jnp.dot(act, wd_ref[...], preferred_element_type=F32)

    @pl.when(k == 0)
    def _():
        o_ref[...] = part

    @pl.when(k > 0)
    def _():
        o_ref[...] += part

    @pl.when(k == pl.num_programs(1) - 1)
    def _():
        f = o_ref[...]
        o_ref[...] = x_ref[...] + f * lax.rsqrt(jnp.mean(f * f, axis=-1, keepdims=True) + EPS) * gpost_ref[...]


def _ffn(x2d, gpre, wg, wu, wd, gpost):
    rows = x2d.shape[0]
    tm = 2 * MXU_DIM
    tf = GROUP_W
    return pl.pallas_call(
        _ffn_kernel,
        grid=(rows // tm, FFN_HIDDEN // tf),
        in_specs=[
            pl.BlockSpec((tm, D_MODEL), lambda i, k: (i, 0)),
            pl.BlockSpec((1, D_MODEL), lambda i, k: (0, 0)),
            pl.BlockSpec((D_MODEL, tf), lambda i, k: (0, k)),
            pl.BlockSpec((D_MODEL, tf), lambda i, k: (0, k)),
            pl.BlockSpec((tf, D_MODEL), lambda i, k: (k, 0)),
            pl.BlockSpec((1, D_MODEL), lambda i, k: (0, 0)),
        ],
        out_specs=pl.BlockSpec((tm, D_MODEL), lambda i, k: (i, 0)),
        out_shape=jax.ShapeDtypeStruct((rows, D_MODEL), F32),
        scratch_shapes=[pltpu.VMEM((tm, D_MODEL), BF16)],
        compiler_params=_params("parallel", "arbitrary"),
        name="ffn",
    )(x2d, gpre.reshape(1, D_MODEL), wg, wu, wd, gpost.reshape(1, D_MODEL))


def kernel(x, meta_tokens, norm_mix_pre, w_in, diff_lambda_q1, diff_lambda_k1, diff_lambda_q2, diff_lambda_k2, diff_norm, sb_norm, ssm_a_re, ssm_a_im, ssm_log_dt, ssm_b_re, ssm_b_im, ssm_c_re, ssm_c_im, ssm_d, ssm_w_glu, ssm_b_glu, ssm_norm, ret_norm, w_out, norm_mix_post, norm_ffn_pre, w_ffn_gate, w_ffn_up, w_ffn_down, norm_ffn_post):
    bsz, seq, d_model = x.shape
    assert d_model == D_MODEL and meta_tokens.shape == (N_META, D_MODEL)
    depth = w_in.shape[0]
    n_tok = N_META + seq
    lp = _padded_len(n_tok)
    pad_front = lp - n_tok

    meta = jnp.broadcast_to(meta_tokens[None].astype(x.dtype), (bsz, N_META, D_MODEL))
    h = jnp.concatenate([jnp.zeros((bsz, pad_front, D_MODEL), x.dtype), meta, x], axis=1)
    h = h.reshape(bsz * lp, D_MODEL)

    tabs = _rotary_tables(lp, pad_front)
    tri = (jnp.arange(MXU_DIM)[:, None] >= jnp.arange(MXU_DIM)[None, :]).astype(BF16)

    for l in range(depth):
        lam_init = 0.8 - 0.6 * math.exp(-0.3 * l)
        proj, su_chunks = _in_proj(h, norm_mix_pre[l], w_in[l].astype(BF16), tabs, lp=lp, pad_front=pad_front)
        lam_vecs = jnp.stack([diff_lambda_q1[l], diff_lambda_k1[l], diff_lambda_q2[l], diff_lambda_k2[l]]).astype(F32)
        o_diff = _diff_attention(proj, lam_vecs, diff_norm[l], bsz=bsz, lp=lp, pad_front=pad_front, lam_init=lam_init)
        o_sb = _sb_attention(proj, tri, sb_norm[l], bsz=bsz, lp=lp, pad_front=pad_front)
        mats = _s5_matrices(ssm_a_re[l], ssm_a_im[l], ssm_log_dt[l], ssm_b_re[l], ssm_b_im[l], ssm_c_re[l], ssm_c_im[l])
        o_ssm = _s5_mixer(proj, su_chunks, mats, ssm_d[l], ssm_w_glu[l], ssm_b_glu[l], ssm_norm[l], bsz=bsz, lp=lp)
        o_ret = _retention(proj, ret_norm[l], bsz=bsz, lp=lp)
        h = _out_proj((o_diff, o_sb, o_ssm, o_ret), w_out[l].astype(BF16), h, norm_mix_post[l])
        h = _ffn(h, norm_ffn_pre[l], w_ffn_gate[l].astype(BF16), w_ffn_up[l].astype(BF16),
                 w_ffn_down[l].astype(BF16), norm_ffn_post[l])

    return h.reshape(bsz, lp, D_MODEL)[:, pad_front + N_META:]
```

```python
import functools
import math

import jax
import jax.numpy as jnp
from jax import lax
from jax.experimental import pallas as pl
from jax.experimental.pallas import tpu as pltpu

F32 = jnp.float32
BF16 = jnp.bfloat16

D_MODEL = 2048
N_META = 16
GROUP_W = D_MODEL // 4
DIFF_HEADS = 4
DIFF_QK = 64
DIFF_V = GROUP_W // DIFF_HEADS
SB_HEADS = 8
SB_DIM = GROUP_W // SB_HEADS
SSM_GROUP = 16
SSM_NG = GROUP_W // SSM_GROUP
SSM_STATE = 64
RET_HEADS = 4
RET_DIM = GROUP_W // RET_HEADS
FFN_HIDDEN = -(-8 * D_MODEL // (3 * 256)) * 256
IN_COLS = 11 * GROUP_W
ROPE_THETA = 500000.0
ROPE_DIM = DIFF_QK // 4
RET_THETA = 10000.0
NEG_INF = -1e30
EPS = 1e-6
LOG2E = 1.4426950408889634

LANES = 128
MXU_DIM = 256
SEQ_TILE = 768
SSM_CHUNK = 16
DIFF_HEADS_PER_STEP = 2
SB_BLOCKS_PER_STEP = 2
VMEM_LIMIT = 52 * 1024 * 1024
VMEM_HEADROOM = 4 * 1024 * 1024

COL_DQ, COL_DK, COL_DV, COL_SQ, COL_SK, COL_SV, COL_SU, COL_RQ, COL_RK, COL_RV, COL_RG = range(11)


def _padded_len(n_tok):
    lp = -(-n_tok // MXU_DIM) * MXU_DIM
    while lp % SEQ_TILE:
        lp += MXU_DIM
    return lp


def _params(*sem):
    return pltpu.CompilerParams(dimension_semantics=sem, vmem_limit_bytes=VMEM_LIMIT)


def _in_proj_kernel(x_ref, g_ref, w_ref, dc_ref, dsa_ref, dsb_ref, rc_ref, rs_ref, o_ref, su_ref, h_scr, su_scr,
                    *, tm, tiles_per_batch, pad_front):
    i = pl.program_id(0)
    j = pl.program_id(1)

    @pl.when(j == 0)
    def _():
        x = x_ref[...]
        y = x * lax.rsqrt(jnp.mean(x * x, axis=-1, keepdims=True) + EPS) * g_ref[...]
        pos = (i % tiles_per_batch) * tm + lax.broadcasted_iota(jnp.int32, (tm, 1), 0)
        h_scr[...] = jnp.where(pos >= pad_front, y, 0.0).astype(BF16)

    acc = jnp.dot(h_scr[...], w_ref[0], preferred_element_type=F32)

    is_diff = (j == COL_DQ) | (j == COL_DK)
    is_ret = (j == COL_RQ) | (j == COL_RK)

    @pl.when(is_diff)
    def _():
        scale = jnp.where(j == COL_DQ, LOG2E * DIFF_QK ** -0.5, 1.0).astype(F32)
        c, sa, sb = dc_ref[...], dsa_ref[...], dsb_ref[...]
        for n in range(GROUP_W // LANES):
            a = acc[:, n * LANES:(n + 1) * LANES]
            r = a * c + pltpu.roll(a, LANES - ROPE_DIM // 2, 1) * sa + pltpu.roll(a, ROPE_DIM // 2, 1) * sb
            o_ref[:, n * LANES:(n + 1) * LANES] = (r * scale).astype(BF16)

    @pl.when(is_ret)
    def _():
        scale = jnp.where(j == COL_RK, RET_DIM ** -0.5, 1.0).astype(F32)
        c, s = rc_ref[...], rs_ref[...]
        for n in range(GROUP_W // LANES):
            a = acc[:, n * LANES:(n + 1) * LANES]
            r = a * c + pltpu.roll(a, RET_DIM // 2, 1) * s
            o_ref[:, n * LANES:(n + 1) * LANES] = (r * scale).astype(BF16)

    @pl.when(jnp.logical_not(is_diff | is_ret))
    def _():
        scale = jnp.where(j == COL_SQ, LOG2E * SB_DIM ** -0.5, 1.0).astype(F32)
        o_ref[...] = (acc * scale).astype(BF16)

    @pl.when(j == COL_SU)
    def _():
        for n in range(GROUP_W // LANES):
            su_scr[n] = acc[:, n * LANES:(n + 1) * LANES]
            for s in range(SSM_CHUNK):
                piece = su_scr[n, pl.ds(s, tm // SSM_CHUNK, stride=SSM_CHUNK), :]
                su_ref[n, :, s * LANES:(s + 1) * LANES] = piece.astype(BF16)


def _in_proj(x2d, gain, w_all, layer, tabs, *, lp, pad_front):
    rows = x2d.shape[0]
    tm = SEQ_TILE
    tiles_per_batch = lp // tm
    n_col = IN_COLS // GROUP_W
    tab_spec = pl.BlockSpec((tm, LANES), lambda i, j: (i % tiles_per_batch, 0))
    return pl.pallas_call(
        functools.partial(_in_proj_kernel, tm=tm, tiles_per_batch=tiles_per_batch, pad_front=pad_front),
        grid=(rows // tm, n_col),
        in_specs=[
            pl.BlockSpec((tm, D_MODEL), lambda i, j: (i, 0)),
            pl.BlockSpec((1, D_MODEL), lambda i, j: (0, 0)),
            pl.BlockSpec((1, D_MODEL, GROUP_W), lambda i, j: (layer, 0, j)),
            tab_spec, tab_spec, tab_spec, tab_spec, tab_spec,
        ],
        out_specs=[pl.BlockSpec((tm, GROUP_W), lambda i, j: (i, j)),
                   pl.BlockSpec((GROUP_W // LANES, tm // SSM_CHUNK, SSM_CHUNK * LANES), lambda i, j: (0, i, 0))],
        out_shape=[jax.ShapeDtypeStruct((rows, IN_COLS), BF16),
                   jax.ShapeDtypeStruct((GROUP_W // LANES, rows // SSM_CHUNK, SSM_CHUNK * LANES), BF16)],
        scratch_shapes=[pltpu.VMEM((tm, D_MODEL), BF16), pltpu.VMEM((GROUP_W // LANES, tm, LANES), F32)],
        compiler_params=_params("parallel", "arbitrary"),
        name="in_proj",
    )(x2d, gain.reshape(1, D_MODEL), w_all, *tabs)


def _rotary_tables(lp, pad_front):
    pos = (jnp.arange(lp, dtype=jnp.int32) - pad_front).astype(F32)[:, None]
    lane = jnp.arange(LANES)
    half = ROPE_DIM // 2
    inv = 1.0 / (ROPE_THETA ** (jnp.arange(half, dtype=F32) * (2.0 / ROPE_DIM)))
    ang = pos * inv[None, :]
    d = lane % DIFF_QK
    cos_l = jnp.cos(ang)[:, d % half]
    sin_l = jnp.sin(ang)[:, d % half]
    dc = jnp.where(d < ROPE_DIM, cos_l, 1.0)
    dsa = jnp.where(d < half, -sin_l, 0.0)
    dsb = jnp.where((d >= half) & (d < ROPE_DIM), sin_l, 0.0)
    rhalf = RET_DIM // 2
    rinv = 1.0 / (RET_THETA ** (jnp.arange(rhalf, dtype=F32) * (2.0 / RET_DIM)))
    rang = pos * rinv[None, :]
    r = lane % RET_DIM
    rc = jnp.cos(rang)[:, r % rhalf]
    rs = jnp.where(r < rhalf, -1.0, 1.0) * jnp.sin(rang)[:, r % rhalf]
    return dc, dsa, dsb, rc, rs


def _tile_pairs(nt, descending):
    pairs = [(i, j) for i in range(nt) for j in (range(i, -1, -1) if descending else range(i + 1))]
    return jnp.array(list(zip(*pairs)), dtype=jnp.int32)


def _dispatch_by_mask_kind(i, j, pad_front, step):
    has_pad = pad_front > 0

    @pl.when((j != i) & (j != 0))
    def _():
        step(causal=False, pad=False)

    @pl.when((j == i) & (i != 0))
    def _():
        step(causal=True, pad=False)

    @pl.when((j == 0) & (i != 0))
    def _():
        step(causal=False, pad=has_pad)

    @pl.when(i == 0)
    def _():
        step(causal=True, pad=has_pad)


def _diff_attn_kernel(ij_ref, lam_ref, q_ref, k_ref, v_ref, gn_ref, o_ref, m_scr, l_scr, acc_scr,
                      *, t, pad_front, lam_init):
    pair = pl.program_id(2)
    i = ij_ref[0, pair]
    j = ij_ref[1, pair]

    @pl.when(j == 0)
    def _():
        m_scr[...] = jnp.full(m_scr.shape, NEG_INF, F32)
        l_scr[...] = jnp.zeros(l_scr.shape, F32)
        acc_scr[...] = jnp.zeros(acc_scr.shape, F32)

    def step(causal, pad):
        mask = None
        if causal or pad:
            row = lax.broadcasted_iota(jnp.int32, (t, t), 0)
            col = lax.broadcasted_iota(jnp.int32, (t, t), 1)
            if causal:
                mask = col <= row
            if pad:
                mask = (col >= pad_front) if mask is None else mask & (col >= pad_front)
        lane = lax.broadcasted_iota(jnp.int32, (t, LANES), 1)
        scores, values = [], []
        for hd in range(DIFF_HEADS_PER_STEP):
            lanes = slice(hd * LANES, (hd + 1) * LANES)
            q, k, v = q_ref[:, lanes], k_ref[:, lanes], v_ref[:, lanes]
            values.append(jnp.concatenate([v, jnp.ones_like(v)], axis=1))
            for c in range(2):
                qc = jnp.where((lane >= c * DIFF_QK) & (lane < (c + 1) * DIFF_QK), q, jnp.zeros_like(q))
                s = lax.dot_general(qc, k, (((1,), (1,)), ((), ())), preferred_element_type=F32)
                if mask is not None:
                    s = jnp.where(mask, s, NEG_INF)
                scores.append(s)
        for n, s in enumerate(scores):
            m_prev = m_scr[n]
            m_new = jnp.maximum(m_prev, jnp.max(s, axis=-1, keepdims=True))
            alpha = jnp.exp2(m_prev - m_new)
            p = jnp.exp2(s - jnp.tile(m_new, (1, t // LANES)))
            pv = jnp.dot(p.astype(BF16), values[n // 2], preferred_element_type=F32)
            acc_scr[n] = alpha * acc_scr[n] + pv[:, :DIFF_V]
            l_scr[n] = alpha * l_scr[n] + pv[:, DIFF_V:]
            m_scr[n] = m_new

    _dispatch_by_mask_kind(i, j, pad_front, step)

    @pl.when(j == i)
    def _():
        lv = lam_ref[...]
        lam = (jnp.exp(jnp.sum(lv[0:1] * lv[1:2], axis=-1, keepdims=True))
               - jnp.exp(jnp.sum(lv[2:3] * lv[3:4], axis=-1, keepdims=True)) + lam_init)
        for hd in range(DIFF_HEADS_PER_STEP):
            a, b = 2 * hd, 2 * hd + 1
            o = acc_scr[a] / l_scr[a] - lam * (acc_scr[b] / l_scr[b])
            y = o * lax.rsqrt(jnp.mean(o * o, axis=-1, keepdims=True) + EPS) * gn_ref[...]
            o_ref[:, hd * LANES:(hd + 1) * LANES] = (y * (1.0 - lam_init)).astype(BF16)


def _diff_attention(proj, lam_vecs, gn, *, bsz, lp, pad_front, lam_init):
    t = SEQ_TILE
    nt = lp // t
    ij = _tile_pairs(nt, descending=False)
    hs = DIFF_HEADS_PER_STEP
    w = hs * LANES
    grid_spec = pltpu.PrefetchScalarGridSpec(
        num_scalar_prefetch=1,
        grid=(bsz, DIFF_HEADS // hs, ij.shape[1]),
        in_specs=[
            pl.BlockSpec((4, DIFF_QK), lambda b, h, p, ij: (0, 0)),
            pl.BlockSpec((t, w), lambda b, h, p, ij: (b * nt + ij[0, p], COL_DQ * GROUP_W // w + h)),
            pl.BlockSpec((t, w), lambda b, h, p, ij: (b * nt + ij[1, p], COL_DK * GROUP_W // w + h)),
            pl.BlockSpec((t, w), lambda b, h, p, ij: (b * nt + ij[1, p], COL_DV * GROUP_W // w + h)),
            pl.BlockSpec((1, DIFF_V), lambda b, h, p, ij: (0, 0)),
        ],
        out_specs=pl.BlockSpec((t, w), lambda b, h, p, ij: (b * nt + ij[0, p], h)),
        scratch_shapes=[pltpu.VMEM((2 * hs, t, LANES), F32), pltpu.VMEM((2 * hs, t, LANES), F32),
                        pltpu.VMEM((2 * hs, t, DIFF_V), F32)],
    )
    return pl.pallas_call(
        functools.partial(_diff_attn_kernel, t=t, pad_front=pad_front, lam_init=lam_init),
        grid_spec=grid_spec,
        out_shape=jax.ShapeDtypeStruct((bsz * lp, GROUP_W), BF16),
        compiler_params=_params("parallel", "parallel", "arbitrary"),
        name="diff_attn",
    )(ij, lam_vecs, proj, proj, proj, gn.reshape(1, DIFF_V))


def _sb_attn_kernel(ij_ref, q_ref, k_ref, v_ref, tri_ref, gn_ref, o_ref, run_scr, acc_scr,
                    *, t, sub, pad_front):
    pair = pl.program_id(2)
    i = ij_ref[0, pair]
    j = ij_ref[1, pair]

    @pl.when(j == i)
    def _():
        run_scr[...] = jnp.zeros(run_scr.shape, F32)
        acc_scr[...] = jnp.zeros(acc_scr.shape, F32)

    def block(qh, hh, lanes, r0, c0, mask):
        z = lax.dot_general(qh[r0:], k_ref[c0:c0 + sub, lanes], (((1,), (1,)), ((), ())),
                            preferred_element_type=F32)
        sign_bit = jnp.uint32(0x80000000)
        neg_abs = lax.bitcast_convert_type(lax.bitcast_convert_type(z, jnp.uint32) | sign_bit, F32)
        sp = jnp.maximum(z, 0.0) + jnp.log(1.0 + jnp.exp2(neg_abs)) * LOG2E
        if mask is not None:
            sp = jnp.where(mask, sp, 0.0)
        tail = jnp.dot(sp.astype(BF16), tri_ref[...], preferred_element_type=F32)
        run = run_scr[hh, r0:, :]
        log_w = z - tail - jnp.tile(run, (1, sub // LANES))
        if mask is not None:
            log_w = jnp.where(mask, log_w, NEG_INF)
        w = jnp.exp2(log_w)
        acc_scr[hh, r0:, :] += jnp.dot(w.astype(BF16), v_ref[c0:c0 + sub, lanes], preferred_element_type=F32)
        run_scr[hh, r0:, :] = run + jnp.sum(sp, axis=-1, keepdims=True)

    heads_per_block = LANES // SB_DIM

    def step(causal, pad):
        lane = lax.broadcasted_iota(jnp.int32, (t, LANES), 1)
        first_key = pad_front if pad else 0
        for blk in range(SB_BLOCKS_PER_STEP):
            lanes = slice(blk * LANES, (blk + 1) * LANES)
            q = q_ref[:, lanes]
            for hh in range(heads_per_block):
                qh = jnp.where((lane >= hh * SB_DIM) & (lane < (hh + 1) * SB_DIM), q, jnp.zeros_like(q))
                for c0 in reversed(range(0, t, sub)):
                    if c0 + sub <= first_key:
                        continue
                    r0 = c0 if causal else 0
                    mask = None
                    if causal or c0 < first_key:
                        row = r0 + lax.broadcasted_iota(jnp.int32, (t - r0, sub), 0)
                        col = c0 + lax.broadcasted_iota(jnp.int32, (t - r0, sub), 1)
                        if causal:
                            mask = col < row
                        if c0 < first_key:
                            mask = (col >= first_key) if mask is None else mask & (col >= first_key)
                    block(qh, blk * heads_per_block + hh, lanes, r0, c0, mask)

    _dispatch_by_mask_kind(i, j, pad_front, step)

    @pl.when(j == 0)
    def _():
        lane = lax.broadcasted_iota(jnp.int32, (t, LANES), 1)
        for blk in range(SB_BLOCKS_PER_STEP):
            y = jnp.zeros((t, LANES), F32)
            for hh in range(heads_per_block):
                in_head = (lane >= hh * SB_DIM) & (lane < (hh + 1) * SB_DIM)
                o = jnp.where(in_head, acc_scr[blk * heads_per_block + hh], 0.0)
                ms = jnp.sum(o * o, axis=-1, keepdims=True) * (1.0 / SB_DIM)
                y = y + o * lax.rsqrt(ms + EPS)
            o_ref[:, blk * LANES:(blk + 1) * LANES] = (y * gn_ref[...]).astype(BF16)


def _sb_attention(proj, tri, gn, *, bsz, lp, pad_front):
    t = SEQ_TILE
    nt = lp // t
    w = SB_BLOCKS_PER_STEP * LANES
    n_slot = SB_BLOCKS_PER_STEP * (LANES // SB_DIM)
    ij = _tile_pairs(nt, descending=True)
    grid_spec = pltpu.PrefetchScalarGridSpec(
        num_scalar_prefetch=1,
        grid=(bsz, GROUP_W // w, ij.shape[1]),
        in_specs=[
            pl.BlockSpec((t, w), lambda b, h, p, ij: (b * nt + ij[0, p], COL_SQ * GROUP_W // w + h)),
            pl.BlockSpec((t, w), lambda b, h, p, ij: (b * nt + ij[1, p], COL_SK * GROUP_W // w + h)),
            pl.BlockSpec((t, w), lambda b, h, p, ij: (b * nt + ij[1, p], COL_SV * GROUP_W // w + h)),
            pl.BlockSpec((MXU_DIM, MXU_DIM), lambda b, h, p, ij: (0, 0)),
            pl.BlockSpec((1, LANES), lambda b, h, p, ij: (0, 0)),
        ],
        out_specs=pl.BlockSpec((t, w), lambda b, h, p, ij: (b * nt + ij[0, p], h)),
        scratch_shapes=[pltpu.VMEM((n_slot, t, LANES), F32), pltpu.VMEM((n_slot, t, LANES), F32)],
    )
    return pl.pallas_call(
        functools.partial(_sb_attn_kernel, t=t, sub=MXU_DIM, pad_front=pad_front),
        grid_spec=grid_spec,
        out_shape=jax.ShapeDtypeStruct((bsz * lp, GROUP_W), BF16),
        compiler_params=_params("parallel", "parallel", "arbitrary"),
        name="sb_attn",
    )(ij, proj, proj, proj, tri, jnp.tile(gn.reshape(1, SB_DIM), (1, LANES // SB_DIM)))


def _s5_intra_kernel(u_ref, d_ref, y_ref):
    zero = jnp.zeros((LANES, LANES), BF16)
    for t0 in range(0, SSM_CHUNK, 2):
        w = jnp.concatenate(
            [jnp.concatenate([d_ref[0, t0 - s] if s <= t0 else zero, d_ref[0, t0 + 1 - s]], axis=1)
             for s in range(t0 + 2)], axis=0)
        y_ref[0, :, t0 * LANES:(t0 + 2) * LANES] = jnp.dot(
            u_ref[0, :, :(t0 + 2) * LANES], w, preferred_element_type=F32)


def _s5_state_kernel(u_ref, e_ref, o_ref):
    u = u_ref[0]
    cols = 4 * SSM_STATE
    row = lax.broadcasted_iota(jnp.int32, (LANES, cols), 0)
    for g in range(LANES // SSM_GROUP):
        in_group = (row >= g * SSM_GROUP) & (row < (g + 1) * SSM_GROUP)
        w = jnp.concatenate([jnp.where(in_group, e_ref[0, s], jnp.zeros((LANES, cols), BF16))
                             for s in range(SSM_CHUNK)], axis=0)
        o_ref[:, g * cols:(g + 1) * cols] = jnp.dot(u, w, preferred_element_type=F32)


def _s5_scan_kernel(s_ref, a1_ref, a2_ref, x_ref, *, n_chunk, bsz, groups):
    a1 = a1_ref[...]
    a2 = a2_ref[...]
    w = groups * LANES

    def body(c, carry):
        new = []
        for b in range(bsz):
            x, xs = carry[b]
            row = s_ref[pl.ds(b * n_chunk + c, 1), :]
            x_ref[pl.ds(b * n_chunk + c, 1), :] = x
            s_x = jnp.concatenate([row[:, g * 2 * LANES:g * 2 * LANES + LANES] for g in range(groups)], axis=1)
            s_xs = jnp.concatenate([row[:, g * 2 * LANES + LANES:(g + 1) * 2 * LANES] for g in range(groups)], axis=1)
            new.append((a1 * x + a2 * xs + s_x, a1 * xs - a2 * x + s_xs))
        return tuple(new)

    zero = jnp.zeros((1, w), F32)
    lax.fori_loop(0, n_chunk, body, tuple((zero, zero) for _ in range(bsz)))


def _s5_cross_kernel(yi_ref, x_ref, c_ref, y_ref):
    x = x_ref[...].astype(BF16)
    for t0 in range(0, SSM_CHUNK, 2):
        w = jnp.concatenate([c_ref[0, t0], c_ref[0, t0 + 1]], axis=1)
        cols = slice(t0 * LANES, (t0 + 2) * LANES)
        y_ref[0, :, cols] = yi_ref[0, :, cols] + jnp.dot(x, w, preferred_element_type=F32)


def _s5_out_kernel(y_ref, u_ref, d_ref, w_ref, b_ref, gn_ref, o_ref, y_scr):
    n_chunk = y_ref.shape[1]
    n_blk = GROUP_W // LANES
    for n in range(n_blk):
        for s in range(SSM_CHUNK):
            y_scr[n, pl.ds(s, n_chunk, stride=SSM_CHUNK), :] = y_ref[n, :, s * LANES:(s + 1) * LANES]
    y = jnp.concatenate([y_scr[n] for n in range(n_blk)], axis=1) + d_ref[...] * u_ref[...].astype(F32)
    g = 0.5 * y * (1.0 + jnp.tanh(math.sqrt(2.0 / math.pi) * (y + 0.044715 * (y * y * y))))
    gate = jnp.dot(g.astype(BF16), w_ref[...], preferred_element_type=F32) + b_ref[...]
    o = g * (1.0 / (1.0 + jnp.exp(-gate)))
    o_ref[...] = (o * lax.rsqrt(jnp.mean(o * o, axis=-1, keepdims=True) + EPS) * gn_ref[...]).astype(BF16)


def _s5_matrices(a_re, a_im, log_dt, b_re, b_im, c_re, c_im):
    hp = lax.Precision.HIGHEST
    t = SSM_CHUNK
    dt = jnp.exp(log_dt.astype(F32))[:, None]
    ar, ai = a_re.astype(F32), a_im.astype(F32)
    dpow = jnp.arange(t + 1, dtype=F32)[:, None, None]
    mag = jnp.exp(dpow * (ar * dt)[None])
    pr = mag * jnp.cos(dpow * (ai * dt)[None])
    pi = mag * jnp.sin(dpow * (ai * dt)[None])
    nr, ni = pr[1] - 1.0, pi[1]
    den = ar * ar + ai * ai
    fr = (nr * ar + ni * ai) / den
    fi = (ni * ar - nr * ai) / den
    br, bi = b_re.astype(F32), b_im.astype(F32)
    bbr = fr[..., None] * br - fi[..., None] * bi
    bbi = fr[..., None] * bi + fi[..., None] * br
    cr, ci = c_re.astype(F32), c_im.astype(F32)
    car = cr[None] * pr[:, :, None, :] - ci[None] * pi[:, :, None, :]
    cai = cr[None] * pi[:, :, None, :] + ci[None] * pr[:, :, None, :]
    kern = (jnp.einsum('dghp,gpk->dghk', car[:t], bbr, precision=hp)
            - jnp.einsum('dghp,gpk->dghk', cai[:t], bbi, precision=hp))
    gl = LANES // SSM_GROUP
    nb = SSM_NG // gl
    place = (jnp.arange(LANES)[None, None, :]
             == jnp.arange(gl)[:, None, None] * SSM_GROUP + jnp.arange(SSM_GROUP)[None, :, None]).astype(F32)
    kern_b = kern.reshape(t, nb, gl, SSM_GROUP, SSM_GROUP)
    d_blocks = jnp.einsum('dbghk,ghl->bdgkl', kern_b, place, precision=hp).reshape(nb, t, LANES, LANES)
    rpow = (t - 1.0) - jnp.arange(t, dtype=F32)[:, None, None]
    rmag = jnp.exp(rpow * (ar * dt)[None])
    rev_r = (rmag * jnp.cos(rpow * (ai * dt)[None]))[:, :, None, :]
    rev_i = (rmag * jnp.sin(rpow * (ai * dt)[None]))[:, :, None, :]
    bbr_t, bbi_t = jnp.swapaxes(bbr, 1, 2)[None], jnp.swapaxes(bbi, 1, 2)[None]
    sre = rev_r * bbr_t - rev_i * bbi_t
    sim = rev_r * bbi_t + rev_i * bbr_t
    st = jnp.concatenate([sre, sim, sim, sre], axis=-1)
    e_blocks = jnp.transpose(st.astype(BF16).reshape(t, nb, LANES, 4 * SSM_STATE), (1, 0, 2, 3))
    cx = jnp.concatenate([car[1:], -cai[1:]], axis=-1)
    cx = cx.reshape(t, nb, gl, SSM_GROUP, 2 * SSM_STATE)
    c_blocks = jnp.einsum('tbghc,ghl->btgcl', cx, place, precision=hp,
                          preferred_element_type=BF16).reshape(nb, t, gl * 2 * SSM_STATE, LANES)
    a1 = jnp.concatenate([pr[t], pr[t]], axis=-1).reshape(1, SSM_NG * LANES)
    a2 = jnp.concatenate([-pi[t], pi[t]], axis=-1).reshape(1, SSM_NG * LANES)
    return d_blocks.astype(BF16), e_blocks.astype(BF16), c_blocks.astype(BF16), a1, a2


def _s5_mixer(proj, u_c, mats, d, w_glu, b_glu, gn, *, bsz, lp):
    d_blocks, e_blocks, c_blocks, a1, a2 = mats
    t = SSM_CHUNK
    nb = GROUP_W // LANES
    gl = LANES // SSM_GROUP
    cl = t * LANES
    sl = gl * 4 * SSM_STATE
    xl = gl * 2 * SSM_STATE
    n_chunk = lp // t
    nc = bsz * n_chunk
    rows = bsz * lp
    rt = nc // 2

    y_intra = pl.pallas_call(
        _s5_intra_kernel,
        grid=(nb, nc // rt),
        in_specs=[
            pl.BlockSpec((1, rt, cl), lambda n, r: (n, r, 0)),
            pl.BlockSpec((1, t, LANES, LANES), lambda n, r: (n, 0, 0, 0)),
        ],
        out_specs=pl.BlockSpec((1, rt, cl), lambda n, r: (n, r, 0)),
        out_shape=jax.ShapeDtypeStruct((nb, nc, cl), F32),
        compiler_params=_params("parallel", "parallel"),
        name="s5_intra",
    )(u_c, d_blocks)

    s_end = pl.pallas_call(
        _s5_state_kernel,
        grid=(nb, nc // rt),
        in_specs=[
            pl.BlockSpec((1, rt, cl), lambda n, r: (n, r, 0)),
            pl.BlockSpec((1, t, LANES, 4 * SSM_STATE), lambda n, r: (n, 0, 0, 0)),
        ],
        out_specs=pl.BlockSpec((rt, sl), lambda n, r: (r, n)),
        out_shape=jax.ShapeDtypeStruct((nc, nb * sl), F32),
        compiler_params=_params("parallel", "parallel"),
        name="s5_state",
    )(u_c, e_blocks)

    x_start = pl.pallas_call(
        functools.partial(_s5_scan_kernel, n_chunk=n_chunk, bsz=bsz, groups=gl),
        grid=(nb,),
        in_specs=[
            pl.BlockSpec((nc, sl), lambda n: (0, n)),
            pl.BlockSpec((1, xl), lambda n: (0, n)),
            pl.BlockSpec((1, xl), lambda n: (0, n)),
        ],
        out_specs=pl.BlockSpec((nc, xl), lambda n: (0, n)),
        out_shape=jax.ShapeDtypeStruct((nc, nb * xl), F32),
        compiler_params=_params("parallel"),
        name="s5_scan",
    )(s_end, a1, a2)

    y_c = pl.pallas_call(
        _s5_cross_kernel,
        grid=(nb, nc // rt),
        in_specs=[
            pl.BlockSpec((1, rt, cl), lambda n, r: (n, r, 0)),
            pl.BlockSpec((rt, xl), lambda n, r: (r, n)),
            pl.BlockSpec((1, t, xl, LANES), lambda n, r: (n, 0, 0, 0)),
        ],
        out_specs=pl.BlockSpec((1, rt, cl), lambda n, r: (n, r, 0)),
        out_shape=jax.ShapeDtypeStruct((nb, nc, cl), F32),
        compiler_params=_params("parallel", "parallel"),
        name="s5_cross",
    )(y_intra, x_start, c_blocks)

    tm = SEQ_TILE
    vec = lambda: pl.BlockSpec((1, GROUP_W), lambda i: (0, 0))
    return pl.pallas_call(
        _s5_out_kernel,
        grid=(rows // tm,),
        in_specs=[
            pl.BlockSpec((nb, tm // t, cl), lambda i: (0, i, 0)),
            pl.BlockSpec((tm, GROUP_W), lambda i: (i, COL_SU)),
            vec(),
            pl.BlockSpec((GROUP_W, GROUP_W), lambda i: (0, 0)),
            vec(), vec(),
        ],
        out_specs=pl.BlockSpec((tm, GROUP_W), lambda i: (i, 0)),
        out_shape=jax.ShapeDtypeStruct((rows, GROUP_W), BF16),
        scratch_shapes=[pltpu.VMEM((nb, tm, LANES), F32)],
        compiler_params=_params("parallel"),
        name="s5_out",
    )(y_c, proj, d.reshape(1, GROUP_W), w_glu.astype(BF16), b_glu.reshape(1, GROUP_W), gn.reshape(1, GROUP_W))


def _retention_kernel(lg_ref, q_ref, k_ref, v_ref, g_ref, gn_ref, o_ref, st_scr, dmat_scr, *, t):
    c = pl.program_id(2)
    lg = lg_ref[0][:, :1]

    @pl.when(c == 0)
    def _():
        st_scr[...] = jnp.zeros(st_scr.shape, F32)
        ti = lax.broadcasted_iota(jnp.int32, (t, t), 0)
        si = lax.broadcasted_iota(jnp.int32, (t, t), 1)
        rel = (ti - si).astype(F32)
        dmat_scr[...] = jnp.where(ti >= si, jnp.exp(lg * jnp.maximum(rel, 0.0)), 0.0)

    q, k, v = q_ref[...], k_ref[...], v_ref[...]
    col = lax.broadcasted_iota(jnp.int32, (t, 1), 0).astype(F32)
    q_decay = jnp.exp(lg * (col + 1.0))
    k_decay = jnp.exp(lg * (t - 1.0 - col))
    chunk_decay = jnp.exp(lg * float(t))

    s = lax.dot_general(q, k, (((1,), (1,)), ((), ())), preferred_element_type=F32) * dmat_scr[...]
    state = st_scr[...]
    o = (jnp.dot(s.astype(BF16), v, preferred_element_type=F32)
         + jnp.dot(q, state.astype(BF16), preferred_element_type=F32) * q_decay)
    kd = (k.astype(F32) * k_decay).astype(BF16)
    st_scr[...] = chunk_decay * state + lax.dot_general(kd, v, (((0,), (0,)), ((), ())),
                                                        preferred_element_type=F32)

    mu = jnp.mean(o, axis=-1, keepdims=True)
    oc = o - mu
    y = oc * lax.rsqrt(jnp.mean(oc * oc, axis=-1, keepdims=True) + EPS) * gn_ref[...]
    gate = g_ref[...].astype(F32)
    o_ref[...] = (y * (gate / (1.0 + jnp.exp(-gate)))).astype(BF16)


def _retention(proj, gn, *, bsz, lp):
    t = SEQ_TILE
    nt = lp // t
    log_gamma = jnp.log1p(-jnp.exp2(-5.0 - jnp.arange(RET_HEADS, dtype=F32)))
    lg = jnp.broadcast_to(log_gamma[:, None, None], (RET_HEADS, 1, LANES))
    blk = lambda off: pl.BlockSpec((t, LANES), lambda b, h, c: (b * nt + c, off * 4 + h))
    return pl.pallas_call(
        functools.partial(_retention_kernel, t=t),
        grid=(bsz, RET_HEADS, nt),
        in_specs=[
            pl.BlockSpec((1, 1, LANES), lambda b, h, c: (h, 0, 0)),
            blk(COL_RQ), blk(COL_RK), blk(COL_RV), blk(COL_RG),
            pl.BlockSpec((1, RET_DIM), lambda b, h, c: (0, 0)),
        ],
        out_specs=pl.BlockSpec((t, LANES), lambda b, h, c: (b * nt + c, h)),
        out_shape=jax.ShapeDtypeStruct((bsz * lp, GROUP_W), BF16),
        scratch_shapes=[pltpu.VMEM((RET_DIM, RET_DIM), F32), pltpu.VMEM((t, t), F32)],
        compiler_params=_params("parallel", "parallel", "arbitrary"),
        name="retention",
    )(lg, proj, proj, proj, proj, gn.reshape(1, RET_DIM))


def _out_proj_kernel(a_ref, b_ref, c_ref, d_ref, w_ref, x_ref, g_ref, o_ref):
    acc = jnp.dot(a_ref[...], w_ref[0], preferred_element_type=F32)
    acc += jnp.dot(b_ref[...], w_ref[1], preferred_element_type=F32)
    acc += jnp.dot(c_ref[...], w_ref[2], preferred_element_type=F32)
    acc += jnp.dot(d_ref[...], w_ref[3], preferred_element_type=F32)
    y = acc * lax.rsqrt(jnp.mean(acc * acc, axis=-1, keepdims=True) + EPS) * g_ref[...]
    o_ref[...] = x_ref[...] + y


def _out_proj(mixes, w_all, layer, x2d, gain):
    rows = x2d.shape[0]
    tm = MXU_DIM
    mix_spec = pl.BlockSpec((tm, GROUP_W), lambda i: (i, 0))
    return pl.pallas_call(
        _out_proj_kernel,
        grid=(rows // tm,),
        in_specs=[
            mix_spec, mix_spec, mix_spec, mix_spec,
            pl.BlockSpec((4, GROUP_W, D_MODEL), lambda i: (layer, 0, 0)),
            pl.BlockSpec((tm, D_MODEL), lambda i: (i, 0)),
            pl.BlockSpec((1, D_MODEL), lambda i: (0, 0)),
        ],
        out_specs=pl.BlockSpec((tm, D_MODEL), lambda i: (i, 0)),
        out_shape=jax.ShapeDtypeStruct((rows, D_MODEL), F32),
        compiler_params=_params("parallel"),
        name="out_proj",
    )(*mixes, w_all.reshape(-1, GROUP_W, D_MODEL), x2d, gain.reshape(1, D_MODEL))


def _ffn_kernel(x_ref, gpre_ref, wg_ref, wu_ref, wd_ref, gpost_ref, o_ref, h_scr):
    k = pl.program_id(1)

    @pl.when(k == 0)
    def _():
        x = x_ref[...]
        h_scr[...] = (x * lax.rsqrt(jnp.mean(x * x, axis=-1, keepdims=True) + EPS) * gpre_ref[...]).astype(BF16)

    h = h_scr[...]
    gate = jnp.dot(h, wg_ref[0], preferred_element_type=F32)
    up = jnp.dot(h, wu_ref[0], preferred_element_type=F32)
    act = (gate / (1.0 + jnp.exp(-gate)) * up).astype(BF16)
    part = jnp.dot(act, wd_ref[0], preferred_element_type=F32)

    @pl.when(k == 0)
    def _():
        o_ref[...] = part

    @pl.when(k > 0)
    def _():
        o_ref[...] += part

    @pl.when(k == pl.num_programs(1) - 1)
    def _():
        f = o_ref[...]
        o_ref[...] = x_ref[...] + f * lax.rsqrt(jnp.mean(f * f, axis=-1, keepdims=True) + EPS) * gpost_ref[...]


def _ffn(x2d, gpre, wg_all, wu_all, wd_all, layer, gpost):
    rows = x2d.shape[0]
    tm = SEQ_TILE
    tf = GROUP_W
    vmem = (2 * 2 * tm * D_MODEL * 4 + tm * D_MODEL * 2 + 2 * 3 * D_MODEL * tf * 2
            + 2 * tm * tf * 4 + tm * tf * 2 + tm * D_MODEL * 4)
    return pl.pallas_call(
        _ffn_kernel,
        grid=(rows // tm, FFN_HIDDEN // tf),
        in_specs=[
            pl.BlockSpec((tm, D_MODEL), lambda i, k: (i, 0)),
            pl.BlockSpec((1, D_MODEL), lambda i, k: (0, 0)),
            pl.BlockSpec((1, D_MODEL, tf), lambda i, k: (layer, 0, k)),
            pl.BlockSpec((1, D_MODEL, tf), lambda i, k: (layer, 0, k)),
            pl.BlockSpec((1, tf, D_MODEL), lambda i, k: (layer, k, 0)),
            pl.BlockSpec((1, D_MODEL), lambda i, k: (0, 0)),
        ],
        out_specs=pl.BlockSpec((tm, D_MODEL), lambda i, k: (i, 0)),
        out_shape=jax.ShapeDtypeStruct((rows, D_MODEL), F32),
        scratch_shapes=[pltpu.VMEM((tm, D_MODEL), BF16)],
        compiler_params=pltpu.CompilerParams(dimension_semantics=("parallel", "arbitrary"),
                                             vmem_limit_bytes=vmem + VMEM_HEADROOM),
        name="ffn",
    )(x2d, gpre.reshape(1, D_MODEL), wg_all, wu_all, wd_all, gpost.reshape(1, D_MODEL))


def kernel(x, meta_tokens, norm_mix_pre, w_in, diff_lambda_q1, diff_lambda_k1, diff_lambda_q2, diff_lambda_k2, diff_norm, sb_norm, ssm_a_re, ssm_a_im, ssm_log_dt, ssm_b_re, ssm_b_im, ssm_c_re, ssm_c_im, ssm_d, ssm_w_glu, ssm_b_glu, ssm_norm, ret_norm, w_out, norm_mix_post, norm_ffn_pre, w_ffn_gate, w_ffn_up, w_ffn_down, norm_ffn_post):
    bsz, seq, d_model = x.shape
    assert d_model == D_MODEL and meta_tokens.shape == (N_META, D_MODEL)
    depth = w_in.shape[0]
    n_tok = N_META + seq
    lp = _padded_len(n_tok)
    pad_front = lp - n_tok

    meta = jnp.broadcast_to(meta_tokens[None].astype(x.dtype), (bsz, N_META, D_MODEL))
    h = jnp.concatenate([jnp.zeros((bsz, pad_front, D_MODEL), x.dtype), meta, x], axis=1)
    h = h.reshape(bsz * lp, D_MODEL)

    tabs = _rotary_tables(lp, pad_front)
    tri = (jnp.arange(MXU_DIM)[:, None] >= jnp.arange(MXU_DIM)[None, :]).astype(BF16)

    w_in_b, w_out_b = w_in.astype(BF16), w_out.astype(BF16)
    w_gate_b, w_up_b, w_down_b = w_ffn_gate.astype(BF16), w_ffn_up.astype(BF16), w_ffn_down.astype(BF16)

    for l in range(depth):
        lam_init = 0.8 - 0.6 * math.exp(-0.3 * l)
        proj, su_chunks = _in_proj(h, norm_mix_pre[l], w_in_b, l, tabs, lp=lp, pad_front=pad_front)
        lam_vecs = jnp.stack([diff_lambda_q1[l], diff_lambda_k1[l], diff_lambda_q2[l], diff_lambda_k2[l]]).astype(F32)
        o_diff = _diff_attention(proj, lam_vecs, diff_norm[l], bsz=bsz, lp=lp, pad_front=pad_front, lam_init=lam_init)
        o_sb = _sb_attention(proj, tri, sb_norm[l], bsz=bsz, lp=lp, pad_front=pad_front)
        mats = _s5_matrices(ssm_a_re[l], ssm_a_im[l], ssm_log_dt[l], ssm_b_re[l], ssm_b_im[l], ssm_c_re[l], ssm_c_im[l])
        o_ssm = _s5_mixer(proj, su_chunks, mats, ssm_d[l], ssm_w_glu[l], ssm_b_glu[l], ssm_norm[l], bsz=bsz, lp=lp)
        o_ret = _retention(proj, ret_norm[l], bsz=bsz, lp=lp)
        h = _out_proj((o_diff, o_sb, o_ssm, o_ret), w_out_b, l, h, norm_mix_post[l])
        h = _ffn(h, norm_ffn_pre[l], w_gate_b, w_up_b, w_down_b, l, norm_ffn_post[l])

    return h.reshape(bsz, lp, D_MODEL)[:, pad_front + N_META:]
```

```python
import functools
import math

import jax
import jax.numpy as jnp
from jax import lax
from jax.experimental import pallas as pl
from jax.experimental.pallas import tpu as pltpu

F32 = jnp.float32
BF16 = jnp.bfloat16

D_MODEL = 2048
N_META = 16
GROUP_W = D_MODEL // 4
DIFF_HEADS = 4
DIFF_QK = 64
DIFF_V = GROUP_W // DIFF_HEADS
SB_HEADS = 8
SB_DIM = GROUP_W // SB_HEADS
SSM_GROUP = 16
SSM_NG = GROUP_W // SSM_GROUP
SSM_STATE = 64
RET_HEADS = 4
RET_DIM = GROUP_W // RET_HEADS
FFN_HIDDEN = -(-8 * D_MODEL // (3 * 256)) * 256
IN_COLS = 11 * GROUP_W
ROPE_THETA = 500000.0
ROPE_DIM = DIFF_QK // 4
RET_THETA = 10000.0
NEG_INF = -1e30
EPS = 1e-6
LOG2E = 1.4426950408889634

LANES = 128
MXU_DIM = 256
SEQ_TILE = 768
SSM_CHUNK = 16
DIFF_HEADS_PER_STEP = 2
SB_BLOCKS_PER_STEP = 2
VMEM_LIMIT = 52 * 1024 * 1024
VMEM_HEADROOM = 4 * 1024 * 1024

COL_DQ, COL_DK, COL_DV, COL_SQ, COL_SK, COL_SV, COL_SU, COL_RQ, COL_RK, COL_RV, COL_RG = range(11)


def _padded_len(n_tok):
    lp = -(-n_tok // MXU_DIM) * MXU_DIM
    while lp % SEQ_TILE:
        lp += MXU_DIM
    return lp


def _params(*sem):
    return pltpu.CompilerParams(dimension_semantics=sem, vmem_limit_bytes=VMEM_LIMIT)


def _in_proj_kernel(x_ref, g_ref, w_ref, dc_ref, dsa_ref, dsb_ref, rc_ref, rs_ref, o_ref, su_ref, h_scr, su_scr,
                    *, tm, tiles_per_batch, pad_front):
    i = pl.program_id(0)
    j = pl.program_id(1)

    @pl.when(j == 0)
    def _():
        x = x_ref[...]
        y = x * lax.rsqrt(jnp.mean(x * x, axis=-1, keepdims=True) + EPS) * g_ref[...]
        pos = (i % tiles_per_batch) * tm + lax.broadcasted_iota(jnp.int32, (tm, 1), 0)
        h_scr[...] = jnp.where(pos >= pad_front, y, 0.0).astype(BF16)

    acc = jnp.dot(h_scr[...], w_ref[0], preferred_element_type=F32)
    plain_scale = jnp.where(j == COL_SQ, LOG2E * SB_DIM ** -0.5, 1.0).astype(F32)
    o_ref[...] = (acc * plain_scale).astype(BF16)

    is_diff = (j == COL_DQ) | (j == COL_DK)
    is_ret = (j == COL_RQ) | (j == COL_RK)

    @pl.when(is_diff)
    def _():
        scale = jnp.where(j == COL_DQ, LOG2E * DIFF_QK ** -0.5, 1.0).astype(F32)
        c, sa, sb = dc_ref[...], dsa_ref[...], dsb_ref[...]
        for n in range(GROUP_W // LANES):
            a = acc[:, n * LANES:(n + 1) * LANES]
            r = a * c + pltpu.roll(a, LANES - ROPE_DIM // 2, 1) * sa + pltpu.roll(a, ROPE_DIM // 2, 1) * sb
            o_ref[:, n * LANES:(n + 1) * LANES] = (r * scale).astype(BF16)

    @pl.when(is_ret)
    def _():
        scale = jnp.where(j == COL_RK, RET_DIM ** -0.5, 1.0).astype(F32)
        c, s = rc_ref[...], rs_ref[...]
        for n in range(GROUP_W // LANES):
            a = acc[:, n * LANES:(n + 1) * LANES]
            r = a * c + pltpu.roll(a, RET_DIM // 2, 1) * s
            o_ref[:, n * LANES:(n + 1) * LANES] = (r * scale).astype(BF16)

    @pl.when(j == COL_SU)
    def _():
        for n in range(GROUP_W // LANES):
            su_scr[n] = acc[:, n * LANES:(n + 1) * LANES]
            for s in range(SSM_CHUNK):
                piece = su_scr[n, pl.ds(s, tm // SSM_CHUNK, stride=SSM_CHUNK), :]
                su_ref[n, :, s * LANES:(s + 1) * LANES] = piece.astype(BF16)


def _in_proj(x2d, gain, w_all, layer, tabs, *, lp, pad_front):
    rows = x2d.shape[0]
    tm = SEQ_TILE
    tiles_per_batch = lp // tm
    n_col = IN_COLS // GROUP_W
    tab_spec = pl.BlockSpec((tm, LANES), lambda i, j: (i % tiles_per_batch, 0))
    return pl.pallas_call(
        functools.partial(_in_proj_kernel, tm=tm, tiles_per_batch=tiles_per_batch, pad_front=pad_front),
        grid=(rows // tm, n_col),
        in_specs=[
            pl.BlockSpec((tm, D_MODEL), lambda i, j: (i, 0)),
            pl.BlockSpec((1, D_MODEL), lambda i, j: (0, 0)),
            pl.BlockSpec((1, D_MODEL, GROUP_W), lambda i, j: (layer, 0, j)),
            tab_spec, tab_spec, tab_spec, tab_spec, tab_spec,
        ],
        out_specs=[pl.BlockSpec((tm, GROUP_W), lambda i, j: (i, j)),
                   pl.BlockSpec((GROUP_W // LANES, tm // SSM_CHUNK, SSM_CHUNK * LANES), lambda i, j: (0, i, 0))],
        out_shape=[jax.ShapeDtypeStruct((rows, IN_COLS), BF16),
                   jax.ShapeDtypeStruct((GROUP_W // LANES, rows // SSM_CHUNK, SSM_CHUNK * LANES), BF16)],
        scratch_shapes=[pltpu.VMEM((tm, D_MODEL), BF16), pltpu.VMEM((GROUP_W // LANES, tm, LANES), F32)],
        compiler_params=_params("parallel", "arbitrary"),
        name="in_proj",
    )(x2d, gain.reshape(1, D_MODEL), w_all, *tabs)


def _rotary_tables(lp, pad_front):
    pos = (jnp.arange(lp, dtype=jnp.int32) - pad_front).astype(F32)[:, None]
    lane = jnp.arange(LANES)
    half = ROPE_DIM // 2
    inv = 1.0 / (ROPE_THETA ** (jnp.arange(half, dtype=F32) * (2.0 / ROPE_DIM)))
    ang = pos * inv[None, :]
    d = lane % DIFF_QK
    cos_l = jnp.cos(ang)[:, d % half]
    sin_l = jnp.sin(ang)[:, d % half]
    dc = jnp.where(d < ROPE_DIM, cos_l, 1.0)
    dsa = jnp.where(d < half, -sin_l, 0.0)
    dsb = jnp.where((d >= half) & (d < ROPE_DIM), sin_l, 0.0)
    rhalf = RET_DIM // 2
    rinv = 1.0 / (RET_THETA ** (jnp.arange(rhalf, dtype=F32) * (2.0 / RET_DIM)))
    rang = pos * rinv[None, :]
    r = lane % RET_DIM
    rc = jnp.cos(rang)[:, r % rhalf]
    rs = jnp.where(r < rhalf, -1.0, 1.0) * jnp.sin(rang)[:, r % rhalf]
    return dc, dsa, dsb, rc, rs


def _tile_pairs(nt, descending):
    pairs = [(i, j) for i in range(nt) for j in (range(i, -1, -1) if descending else range(i + 1))]
    return jnp.array(list(zip(*pairs)), dtype=jnp.int32)


def _dispatch_by_mask_kind(i, j, pad_front, step):
    has_pad = pad_front > 0

    @pl.when((j != i) & (j != 0))
    def _():
        step(causal=False, pad=False)

    @pl.when((j == i) & (i != 0))
    def _():
        step(causal=True, pad=False)

    @pl.when((j == 0) & (i != 0))
    def _():
        step(causal=False, pad=has_pad)

    @pl.when(i == 0)
    def _():
        step(causal=True, pad=has_pad)


def _diff_attn_kernel(ij_ref, lam_ref, q_ref, k_ref, v_ref, gn_ref, o_ref, m_scr, l_scr, acc_scr,
                      *, t, pad_front, lam_init):
    pair = pl.program_id(2)
    i = ij_ref[0, pair]
    j = ij_ref[1, pair]

    @pl.when(j == 0)
    def _():
        m_scr[...] = jnp.full(m_scr.shape, NEG_INF, F32)
        l_scr[...] = jnp.zeros(l_scr.shape, F32)
        acc_scr[...] = jnp.zeros(acc_scr.shape, F32)

    def step(causal, pad):
        mask = None
        if causal or pad:
            row = lax.broadcasted_iota(jnp.int32, (t, t), 0)
            col = lax.broadcasted_iota(jnp.int32, (t, t), 1)
            if causal:
                mask = col <= row
            if pad:
                mask = (col >= pad_front) if mask is None else mask & (col >= pad_front)
        lane = lax.broadcasted_iota(jnp.int32, (t, LANES), 1)
        scores, values = [], []
        for hd in range(DIFF_HEADS_PER_STEP):
            lanes = slice(hd * LANES, (hd + 1) * LANES)
            q, k, v = q_ref[:, lanes], k_ref[:, lanes], v_ref[:, lanes]
            values.append(jnp.concatenate([v, jnp.ones_like(v)], axis=1))
            for c in range(2):
                qc = jnp.where((lane >= c * DIFF_QK) & (lane < (c + 1) * DIFF_QK), q, jnp.zeros_like(q))
                s = lax.dot_general(qc, k, (((1,), (1,)), ((), ())), preferred_element_type=F32)
                if mask is not None:
                    s = jnp.where(mask, s, NEG_INF)
                scores.append(s)
        for n, s in enumerate(scores):
            m_prev = m_scr[n]
            m_new = jnp.maximum(m_prev, jnp.max(s, axis=-1, keepdims=True))
            alpha = jnp.exp2(m_prev - m_new)
            p = jnp.exp2(s - jnp.tile(m_new, (1, t // LANES)))
            pv = jnp.dot(p.astype(BF16), values[n // 2], preferred_element_type=F32)
            acc_scr[n] = alpha * acc_scr[n] + pv[:, :DIFF_V]
            l_scr[n] = alpha * l_scr[n] + pv[:, DIFF_V:]
            m_scr[n] = m_new

    _dispatch_by_mask_kind(i, j, pad_front, step)

    @pl.when(j == i)
    def _():
        lv = lam_ref[...]
        lam = (jnp.exp(jnp.sum(lv[0:1] * lv[1:2], axis=-1, keepdims=True))
               - jnp.exp(jnp.sum(lv[2:3] * lv[3:4], axis=-1, keepdims=True)) + lam_init)
        for hd in range(DIFF_HEADS_PER_STEP):
            a, b = 2 * hd, 2 * hd + 1
            o = acc_scr[a] / l_scr[a] - lam * (acc_scr[b] / l_scr[b])
            y = o * lax.rsqrt(jnp.mean(o * o, axis=-1, keepdims=True) + EPS) * gn_ref[...]
            o_ref[:, hd * LANES:(hd + 1) * LANES] = (y * (1.0 - lam_init)).astype(BF16)


def _diff_attention(proj, lam_vecs, gn, *, bsz, lp, pad_front, lam_init):
    t = SEQ_TILE
    nt = lp // t
    ij = _tile_pairs(nt, descending=False)
    hs = DIFF_HEADS_PER_STEP
    w = hs * LANES
    grid_spec = pltpu.PrefetchScalarGridSpec(
        num_scalar_prefetch=1,
        grid=(bsz, DIFF_HEADS // hs, ij.shape[1]),
        in_specs=[
            pl.BlockSpec((4, DIFF_QK), lambda b, h, p, ij: (0, 0)),
            pl.BlockSpec((t, w), lambda b, h, p, ij: (b * nt + ij[0, p], COL_DQ * GROUP_W // w + h)),
            pl.BlockSpec((t, w), lambda b, h, p, ij: (b * nt + ij[1, p], COL_DK * GROUP_W // w + h)),
            pl.BlockSpec((t, w), lambda b, h, p, ij: (b * nt + ij[1, p], COL_DV * GROUP_W // w + h)),
            pl.BlockSpec((1, DIFF_V), lambda b, h, p, ij: (0, 0)),
        ],
        out_specs=pl.BlockSpec((t, w), lambda b, h, p, ij: (b * nt + ij[0, p], h)),
        scratch_shapes=[pltpu.VMEM((2 * hs, t, LANES), F32), pltpu.VMEM((2 * hs, t, LANES), F32),
                        pltpu.VMEM((2 * hs, t, DIFF_V), F32)],
    )
    return pl.pallas_call(
        functools.partial(_diff_attn_kernel, t=t, pad_front=pad_front, lam_init=lam_init),
        grid_spec=grid_spec,
        out_shape=jax.ShapeDtypeStruct((bsz * lp, GROUP_W), BF16),
        compiler_params=_params("parallel", "parallel", "arbitrary"),
        name="diff_attn",
    )(ij, lam_vecs, proj, proj, proj, gn.reshape(1, DIFF_V))


def _sb_attn_kernel(ij_ref, q_ref, k_ref, v_ref, tri_ref, gn_ref, o_ref, run_scr, acc_scr,
                    *, t, sub, pad_front):
    pair = pl.program_id(2)
    i = ij_ref[0, pair]
    j = ij_ref[1, pair]

    @pl.when(j == i)
    def _():
        run_scr[...] = jnp.zeros(run_scr.shape, F32)
        acc_scr[...] = jnp.zeros(acc_scr.shape, F32)

    def block(qh, hh, lanes, r0, c0, mask):
        z = lax.dot_general(qh[r0:], k_ref[c0:c0 + sub, lanes], (((1,), (1,)), ((), ())),
                            preferred_element_type=F32)
        sign_bit = jnp.uint32(0x80000000)
        neg_abs = lax.bitcast_convert_type(lax.bitcast_convert_type(z, jnp.uint32) | sign_bit, F32)
        sp = jnp.maximum(z, 0.0) + jnp.log(1.0 + jnp.exp2(neg_abs)) * LOG2E
        if mask is not None:
            sp = jnp.where(mask, sp, 0.0)
        tail = jnp.dot(sp.astype(BF16), tri_ref[...], preferred_element_type=F32)
        run = run_scr[hh, r0:, :]
        log_w = z - tail - jnp.tile(run, (1, sub // LANES))
        if mask is not None:
            log_w = jnp.where(mask, log_w, NEG_INF)
        w = jnp.exp2(log_w)
        acc_scr[hh, r0:, :] += jnp.dot(w.astype(BF16), v_ref[c0:c0 + sub, lanes], preferred_element_type=F32)
        run_scr[hh, r0:, :] = run + jnp.sum(sp, axis=-1, keepdims=True)

    heads_per_block = LANES // SB_DIM

    def step(causal, pad):
        lane = lax.broadcasted_iota(jnp.int32, (t, LANES), 1)
        first_key = pad_front if pad else 0
        for blk in range(SB_BLOCKS_PER_STEP):
            lanes = slice(blk * LANES, (blk + 1) * LANES)
            q = q_ref[:, lanes]
            for hh in range(heads_per_block):
                qh = jnp.where((lane >= hh * SB_DIM) & (lane < (hh + 1) * SB_DIM), q, jnp.zeros_like(q))
                for c0 in reversed(range(0, t, sub)):
                    if c0 + sub <= first_key:
                        continue
                    r0 = c0 if causal else 0
                    mask = None
                    if causal or c0 < first_key:
                        row = r0 + lax.broadcasted_iota(jnp.int32, (t - r0, sub), 0)
                        col = c0 + lax.broadcasted_iota(jnp.int32, (t - r0, sub), 1)
                        if causal:
                            mask = col < row
                        if c0 < first_key:
                            mask = (col >= first_key) if mask is None else mask & (col >= first_key)
                    block(qh, blk * heads_per_block + hh, lanes, r0, c0, mask)

    _dispatch_by_mask_kind(i, j, pad_front, step)

    @pl.when(j == 0)
    def _():
        lane = lax.broadcasted_iota(jnp.int32, (t, LANES), 1)
        for blk in range(SB_BLOCKS_PER_STEP):
            y = jnp.zeros((t, LANES), F32)
            for hh in range(heads_per_block):
                in_head = (lane >= hh * SB_DIM) & (lane < (hh + 1) * SB_DIM)
                o = jnp.where(in_head, acc_scr[blk * heads_per_block + hh], 0.0)
                ms = jnp.sum(o * o, axis=-1, keepdims=True) * (1.0 / SB_DIM)
                y = y + o * lax.rsqrt(ms + EPS)
            o_ref[:, blk * LANES:(blk + 1) * LANES] = (y * gn_ref[...]).astype(BF16)


def _sb_attention(proj, tri, gn, *, bsz, lp, pad_front):
    t = SEQ_TILE
    nt = lp // t
    w = SB_BLOCKS_PER_STEP * LANES
    n_slot = SB_BLOCKS_PER_STEP * (LANES // SB_DIM)
    ij = _tile_pairs(nt, descending=True)
    grid_spec = pltpu.PrefetchScalarGridSpec(
        num_scalar_prefetch=1,
        grid=(bsz, GROUP_W // w, ij.shape[1]),
        in_specs=[
            pl.BlockSpec((t, w), lambda b, h, p, ij: (b * nt + ij[0, p], COL_SQ * GROUP_W // w + h)),
            pl.BlockSpec((t, w), lambda b, h, p, ij: (b * nt + ij[1, p], COL_SK * GROUP_W // w + h)),
            pl.BlockSpec((t, w), lambda b, h, p, ij: (b * nt + ij[1, p], COL_SV * GROUP_W // w + h)),
            pl.BlockSpec((MXU_DIM, MXU_DIM), lambda b, h, p, ij: (0, 0)),
            pl.BlockSpec((1, LANES), lambda b, h, p, ij: (0, 0)),
        ],
        out_specs=pl.BlockSpec((t, w), lambda b, h, p, ij: (b * nt + ij[0, p], h)),
        scratch_shapes=[pltpu.VMEM((n_slot, t, LANES), F32), pltpu.VMEM((n_slot, t, LANES), F32)],
    )
    return pl.pallas_call(
        functools.partial(_sb_attn_kernel, t=t, sub=MXU_DIM, pad_front=pad_front),
        grid_spec=grid_spec,
        out_shape=jax.ShapeDtypeStruct((bsz * lp, GROUP_W), BF16),
        compiler_params=_params("parallel", "parallel", "arbitrary"),
        name="sb_attn",
    )(ij, proj, proj, proj, tri, jnp.tile(gn.reshape(1, SB_DIM), (1, LANES // SB_DIM)))


def _s5_chunk_kernel(u_ref, d_ref, e_ref, y_ref, s_ref):
    _s5_intra(u_ref, d_ref, y_ref)
    _s5_state(u_ref, e_ref, s_ref)


def _s5_intra(u_ref, d_ref, y_ref):
    zero = jnp.zeros((LANES, LANES), BF16)
    for t0 in range(0, SSM_CHUNK, 2):
        w = jnp.concatenate(
            [jnp.concatenate([d_ref[0, t0 - s] if s <= t0 else zero, d_ref[0, t0 + 1 - s]], axis=1)
             for s in range(t0 + 2)], axis=0)
        y_ref[0, :, t0 * LANES:(t0 + 2) * LANES] = jnp.dot(
            u_ref[0, :, :(t0 + 2) * LANES], w, preferred_element_type=F32)


def _s5_state(u_ref, e_ref, o_ref):
    u = u_ref[0]
    cols = 4 * SSM_STATE
    row = lax.broadcasted_iota(jnp.int32, (LANES, cols), 0)
    for g in range(LANES // SSM_GROUP):
        in_group = (row >= g * SSM_GROUP) & (row < (g + 1) * SSM_GROUP)
        w = jnp.concatenate([jnp.where(in_group, e_ref[0, s], jnp.zeros((LANES, cols), BF16))
                             for s in range(SSM_CHUNK)], axis=0)
        o_ref[:, g * cols:(g + 1) * cols] = jnp.dot(u, w, preferred_element_type=F32)


def _s5_scan_kernel(s_ref, a1_ref, a2_ref, x_ref, *, n_chunk, bsz, groups):
    a1 = a1_ref[...]
    a2 = a2_ref[...]
    w = groups * LANES

    def body(c, carry):
        new = []
        for b in range(bsz):
            x, xs = carry[b]
            row = s_ref[pl.ds(b * n_chunk + c, 1), :]
            x_ref[pl.ds(b * n_chunk + c, 1), :] = x
            s_x = jnp.concatenate([row[:, g * 2 * LANES:g * 2 * LANES + LANES] for g in range(groups)], axis=1)
            s_xs = jnp.concatenate([row[:, g * 2 * LANES + LANES:(g + 1) * 2 * LANES] for g in range(groups)], axis=1)
            new.append((a1 * x + a2 * xs + s_x, a1 * xs - a2 * x + s_xs))
        return tuple(new)

    zero = jnp.zeros((1, w), F32)
    lax.fori_loop(0, n_chunk, body, tuple((zero, zero) for _ in range(bsz)))


def _s5_cross_kernel(yi_ref, x_ref, c_ref, y_ref):
    x = x_ref[...].astype(BF16)
    for t0 in range(0, SSM_CHUNK, 2):
        w = jnp.concatenate([c_ref[0, t0], c_ref[0, t0 + 1]], axis=1)
        cols = slice(t0 * LANES, (t0 + 2) * LANES)
        y_ref[0, :, cols] = yi_ref[0, :, cols] + jnp.dot(x, w, preferred_element_type=F32)


def _s5_out_kernel(y_ref, u_ref, d_ref, w_ref, b_ref, gn_ref, o_ref, y_scr):
    n_chunk = y_ref.shape[1]
    n_blk = GROUP_W // LANES
    for n in range(n_blk):
        for s in range(SSM_CHUNK):
            y_scr[n, pl.ds(s, n_chunk, stride=SSM_CHUNK), :] = y_ref[n, :, s * LANES:(s + 1) * LANES]
    y = jnp.concatenate([y_scr[n] for n in range(n_blk)], axis=1) + d_ref[...] * u_ref[...].astype(F32)
    g = 0.5 * y * (1.0 + jnp.tanh(math.sqrt(2.0 / math.pi) * (y + 0.044715 * (y * y * y))))
    gate = jnp.dot(g.astype(BF16), w_ref[...], preferred_element_type=F32) + b_ref[...]
    o = g * (1.0 / (1.0 + jnp.exp(-gate)))
    o_ref[...] = (o * lax.rsqrt(jnp.mean(o * o, axis=-1, keepdims=True) + EPS) * gn_ref[...]).astype(BF16)


def _s5_matrices(a_re, a_im, log_dt, b_re, b_im, c_re, c_im):
    hp = lax.Precision.HIGHEST
    t = SSM_CHUNK
    dt = jnp.exp(log_dt.astype(F32))[:, None]
    ar, ai = a_re.astype(F32), a_im.astype(F32)
    dpow = jnp.arange(t + 1, dtype=F32)[:, None, None]
    mag = jnp.exp(dpow * (ar * dt)[None])
    pr = mag * jnp.cos(dpow * (ai * dt)[None])
    pi = mag * jnp.sin(dpow * (ai * dt)[None])
    nr, ni = pr[1] - 1.0, pi[1]
    den = ar * ar + ai * ai
    fr = (nr * ar + ni * ai) / den
    fi = (ni * ar - nr * ai) / den
    br, bi = b_re.astype(F32), b_im.astype(F32)
    bbr = fr[..., None] * br - fi[..., None] * bi
    bbi = fr[..., None] * bi + fi[..., None] * br
    cr, ci = c_re.astype(F32), c_im.astype(F32)
    car = cr[None] * pr[:, :, None, :] - ci[None] * pi[:, :, None, :]
    cai = cr[None] * pi[:, :, None, :] + ci[None] * pr[:, :, None, :]
    kern = (jnp.einsum('dghp,gpk->dghk', car[:t], bbr, precision=hp)
            - jnp.einsum('dghp,gpk->dghk', cai[:t], bbi, precision=hp))
    gl = LANES // SSM_GROUP
    nb = SSM_NG // gl
    place = (jnp.arange(LANES)[None, None, :]
             == jnp.arange(gl)[:, None, None] * SSM_GROUP + jnp.arange(SSM_GROUP)[None, :, None]).astype(F32)
    kern_b = kern.reshape(t, nb, gl, SSM_GROUP, SSM_GROUP)
    d_blocks = jnp.einsum('dbghk,ghl->bdgkl', kern_b, place,
                          preferred_element_type=BF16).reshape(nb, t, LANES, LANES)
    rpow = (t - 1.0) - jnp.arange(t, dtype=F32)[:, None, None]
    rmag = jnp.exp(rpow * (ar * dt)[None])
    rev_r = (rmag * jnp.cos(rpow * (ai * dt)[None]))[:, :, None, :]
    rev_i = (rmag * jnp.sin(rpow * (ai * dt)[None]))[:, :, None, :]
    bbr_t, bbi_t = jnp.swapaxes(bbr, 1, 2)[None], jnp.swapaxes(bbi, 1, 2)[None]
    sre = rev_r * bbr_t - rev_i * bbi_t
    sim = rev_r * bbi_t + rev_i * bbr_t
    st = jnp.concatenate([sre, sim, sim, sre], axis=-1)
    e_blocks = jnp.transpose(st.astype(BF16).reshape(t, nb, LANES, 4 * SSM_STATE), (1, 0, 2, 3))
    cx = jnp.concatenate([car[1:], -cai[1:]], axis=-1)
    cx = cx.reshape(t, nb, gl, SSM_GROUP, 2 * SSM_STATE)
    c_blocks = jnp.einsum('tbghc,ghl->btgcl', cx, place,
                          preferred_element_type=BF16).reshape(nb, t, gl * 2 * SSM_STATE, LANES)
    a1 = jnp.concatenate([pr[t], pr[t]], axis=-1).reshape(1, SSM_NG * LANES)
    a2 = jnp.concatenate([-pi[t], pi[t]], axis=-1).reshape(1, SSM_NG * LANES)
    return d_blocks.astype(BF16), e_blocks.astype(BF16), c_blocks.astype(BF16), a1, a2


def _s5_mixer(proj, u_c, mats, d, w_glu, b_glu, gn, *, bsz, lp):
    d_blocks, e_blocks, c_blocks, a1, a2 = mats
    t = SSM_CHUNK
    nb = GROUP_W // LANES
    gl = LANES // SSM_GROUP
    cl = t * LANES
    sl = gl * 4 * SSM_STATE
    xl = gl * 2 * SSM_STATE
    n_chunk = lp // t
    nc = bsz * n_chunk
    rows = bsz * lp
    rt = nc // 2

    y_intra, s_end = pl.pallas_call(
        _s5_chunk_kernel,
        grid=(nb, nc // rt),
        in_specs=[
            pl.BlockSpec((1, rt, cl), lambda n, r: (n, r, 0)),
            pl.BlockSpec((1, t, LANES, LANES), lambda n, r: (n, 0, 0, 0)),
            pl.BlockSpec((1, t, LANES, 4 * SSM_STATE), lambda n, r: (n, 0, 0, 0)),
        ],
        out_specs=[pl.BlockSpec((1, rt, cl), lambda n, r: (n, r, 0)),
                   pl.BlockSpec((rt, sl), lambda n, r: (r, n))],
        out_shape=[jax.ShapeDtypeStruct((nb, nc, cl), F32),
                   jax.ShapeDtypeStruct((nc, nb * sl), F32)],
        compiler_params=_params("parallel", "parallel"),
        name="s5_chunk",
    )(u_c, d_blocks, e_blocks)

    x_start = pl.pallas_call(
        functools.partial(_s5_scan_kernel, n_chunk=n_chunk, bsz=bsz, groups=gl),
        grid=(nb,),
        in_specs=[
            pl.BlockSpec((nc, sl), lambda n: (0, n)),
            pl.BlockSpec((1, xl), lambda n: (0, n)),
            pl.BlockSpec((1, xl), lambda n: (0, n)),
        ],
        out_specs=pl.BlockSpec((nc, xl), lambda n: (0, n)),
        out_shape=jax.ShapeDtypeStruct((nc, nb * xl), F32),
        compiler_params=_params("parallel"),
        name="s5_scan",
    )(s_end, a1, a2)

    y_c = pl.pallas_call(
        _s5_cross_kernel,
        grid=(nb, nc // rt),
        in_specs=[
            pl.BlockSpec((1, rt, cl), lambda n, r: (n, r, 0)),
            pl.BlockSpec((rt, xl), lambda n, r: (r, n)),
            pl.BlockSpec((1, t, xl, LANES), lambda n, r: (n, 0, 0, 0)),
        ],
        out_specs=pl.BlockSpec((1, rt, cl), lambda n, r: (n, r, 0)),
        out_shape=jax.ShapeDtypeStruct((nb, nc, cl), F32),
        compiler_params=_params("parallel", "parallel"),
        name="s5_cross",
    )(y_intra, x_start, c_blocks)

    tm = SEQ_TILE
    vec = lambda: pl.BlockSpec((1, GROUP_W), lambda i: (0, 0))
    return pl.pallas_call(
        _s5_out_kernel,
        grid=(rows // tm,),
        in_specs=[
            pl.BlockSpec((nb, tm // t, cl), lambda i: (0, i, 0)),
            pl.BlockSpec((tm, GROUP_W), lambda i: (i, COL_SU)),
            vec(),
            pl.BlockSpec((GROUP_W, GROUP_W), lambda i: (0, 0)),
            vec(), vec(),
        ],
        out_specs=pl.BlockSpec((tm, GROUP_W), lambda i: (i, 0)),
        out_shape=jax.ShapeDtypeStruct((rows, GROUP_W), BF16),
        scratch_shapes=[pltpu.VMEM((nb, tm, LANES), F32)],
        compiler_params=_params("parallel"),
        name="s5_out",
    )(y_c, proj, d.reshape(1, GROUP_W), w_glu.astype(BF16), b_glu.reshape(1, GROUP_W), gn.reshape(1, GROUP_W))


def _retention_kernel(lg_ref, q_ref, k_ref, v_ref, g_ref, gn_ref, o_ref, st_scr, dmat_scr, *, t):
    c = pl.program_id(2)
    lg = lg_ref[0][:, :1]

    @pl.when(c == 0)
    def _():
        st_scr[...] = jnp.zeros(st_scr.shape, F32)
        ti = lax.broadcasted_iota(jnp.int32, (t, t), 0)
        si = lax.broadcasted_iota(jnp.int32, (t, t), 1)
        rel = (ti - si).astype(F32)
        dmat_scr[...] = jnp.where(ti >= si, jnp.exp(lg * jnp.maximum(rel, 0.0)), 0.0)

    q, k, v = q_ref[...], k_ref[...], v_ref[...]
    col = lax.broadcasted_iota(jnp.int32, (t, 1), 0).astype(F32)
    q_decay = jnp.exp(lg * (col + 1.0))
    k_decay = jnp.exp(lg * (t - 1.0 - col))
    chunk_decay = jnp.exp(lg * float(t))

    s = lax.dot_general(q, k, (((1,), (1,)), ((), ())), preferred_element_type=F32) * dmat_scr[...]
    state = st_scr[...]
    o = (jnp.dot(s.astype(BF16), v, preferred_element_type=F32)
         + jnp.dot(q, state.astype(BF16), preferred_element_type=F32) * q_decay)
    kd = (k.astype(F32) * k_decay).astype(BF16)
    st_scr[...] = chunk_decay * state + lax.dot_general(kd, v, (((0,), (0,)), ((), ())),
                                                        preferred_element_type=F32)

    mu = jnp.mean(o, axis=-1, keepdims=True)
    oc = o - mu
    y = oc * lax.rsqrt(jnp.mean(oc * oc, axis=-1, keepdims=True) + EPS) * gn_ref[...]
    gate = g_ref[...].astype(F32)
    o_ref[...] = (y * (gate / (1.0 + jnp.exp(-gate)))).astype(BF16)


def _retention(proj, gn, *, bsz, lp):
    t = SEQ_TILE
    nt = lp // t
    log_gamma = jnp.log1p(-jnp.exp2(-5.0 - jnp.arange(RET_HEADS, dtype=F32)))
    lg = jnp.broadcast_to(log_gamma[:, None, None], (RET_HEADS, 1, LANES))
    blk = lambda off: pl.BlockSpec((t, LANES), lambda b, h, c: (b * nt + c, off * 4 + h))
    return pl.pallas_call(
        functools.partial(_retention_kernel, t=t),
        grid=(bsz, RET_HEADS, nt),
        in_specs=[
            pl.BlockSpec((1, 1, LANES), lambda b, h, c: (h, 0, 0)),
            blk(COL_RQ), blk(COL_RK), blk(COL_RV), blk(COL_RG),
            pl.BlockSpec((1, RET_DIM), lambda b, h, c: (0, 0)),
        ],
        out_specs=pl.BlockSpec((t, LANES), lambda b, h, c: (b * nt + c, h)),
        out_shape=jax.ShapeDtypeStruct((bsz * lp, GROUP_W), BF16),
        scratch_shapes=[pltpu.VMEM((RET_DIM, RET_DIM), F32), pltpu.VMEM((t, t), F32)],
        compiler_params=_params("parallel", "parallel", "arbitrary"),
        name="retention",
    )(lg, proj, proj, proj, proj, gn.reshape(1, RET_DIM))


def _out_proj_kernel(a_ref, b_ref, c_ref, d_ref, w_ref, x_ref, g_ref, o_ref):
    acc = jnp.dot(a_ref[...], w_ref[0], preferred_element_type=F32)
    acc += jnp.dot(b_ref[...], w_ref[1], preferred_element_type=F32)
    acc += jnp.dot(c_ref[...], w_ref[2], preferred_element_type=F32)
    acc += jnp.dot(d_ref[...], w_ref[3], preferred_element_type=F32)
    y = acc * lax.rsqrt(jnp.mean(acc * acc, axis=-1, keepdims=True) + EPS) * g_ref[...]
    o_ref[...] = x_ref[...] + y


def _out_proj(mixes, w_all, layer, x2d, gain, *, rows_per_batch, drop_rows=0):
    rows = x2d.shape[0]
    tm = MXU_DIM
    assert drop_rows % tm == 0 and rows_per_batch % tm == 0
    per_batch, skip = rows_per_batch // tm, drop_rows // tm

    def out_idx(i):
        return ((i // per_batch) * (per_batch - skip) + jnp.maximum(i % per_batch - skip, 0), 0)

    mix_spec = pl.BlockSpec((tm, GROUP_W), lambda i: (i, 0))
    return pl.pallas_call(
        _out_proj_kernel,
        grid=(rows // tm,),
        in_specs=[
            mix_spec, mix_spec, mix_spec, mix_spec,
            pl.BlockSpec((4, GROUP_W, D_MODEL), lambda i: (layer, 0, 0)),
            pl.BlockSpec((tm, D_MODEL), lambda i: (i, 0)),
            pl.BlockSpec((1, D_MODEL), lambda i: (0, 0)),
        ],
        out_specs=pl.BlockSpec((tm, D_MODEL), out_idx),
        out_shape=jax.ShapeDtypeStruct((rows // per_batch * (per_batch - skip), D_MODEL), F32),
        compiler_params=_params("arbitrary"),
        name="out_proj",
    )(*mixes, w_all.reshape(-1, GROUP_W, D_MODEL), x2d, gain.reshape(1, D_MODEL))


def _ffn_kernel(x_ref, gpre_ref, wg_ref, wu_ref, wd_ref, gpost_ref, o_ref, h_scr):
    k = pl.program_id(1)

    @pl.when(k == 0)
    def _():
        x = x_ref[...]
        h_scr[...] = (x * lax.rsqrt(jnp.mean(x * x, axis=-1, keepdims=True) + EPS) * gpre_ref[...]).astype(BF16)

    h = h_scr[...]
    gate = jnp.dot(h, wg_ref[0], preferred_element_type=F32)
    up = jnp.dot(h, wu_ref[0], preferred_element_type=F32)
    act = (gate / (1.0 + jnp.exp(-gate)) * up).astype(BF16)
    part = jnp.dot(act, wd_ref[0], preferred_element_type=F32)

    @pl.when(k == 0)
    def _():
        o_ref[...] = part

    @pl.when(k > 0)
    def _():
        o_ref[...] += part

    @pl.when(k == pl.num_programs(1) - 1)
    def _():
        f = o_ref[...]
        o_ref[...] = x_ref[...] + f * lax.rsqrt(jnp.mean(f * f, axis=-1, keepdims=True) + EPS) * gpost_ref[...]


def _ffn(x2d, gpre, wg_all, wu_all, wd_all, layer, gpost):
    rows = x2d.shape[0]
    tm = next(m for m in (SEQ_TILE, 2 * MXU_DIM, MXU_DIM) if rows % m == 0)
    tf = GROUP_W
    vmem = (2 * 2 * tm * D_MODEL * 4 + tm * D_MODEL * 2 + 2 * 3 * D_MODEL * tf * 2
            + 2 * tm * tf * 4 + tm * tf * 2 + tm * D_MODEL * 4)
    return pl.pallas_call(
        _ffn_kernel,
        grid=(rows // tm, FFN_HIDDEN // tf),
        in_specs=[
            pl.BlockSpec((tm, D_MODEL), lambda i, k: (i, 0)),
            pl.BlockSpec((1, D_MODEL), lambda i, k: (0, 0)),
            pl.BlockSpec((1, D_MODEL, tf), lambda i, k: (layer, 0, k)),
            pl.BlockSpec((1, D_MODEL, tf), lambda i, k: (layer, 0, k)),
            pl.BlockSpec((1, tf, D_MODEL), lambda i, k: (layer, k, 0)),
            pl.BlockSpec((1, D_MODEL), lambda i, k: (0, 0)),
        ],
        out_specs=pl.BlockSpec((tm, D_MODEL), lambda i, k: (i, 0)),
        out_shape=jax.ShapeDtypeStruct((rows, D_MODEL), F32),
        scratch_shapes=[pltpu.VMEM((tm, D_MODEL), BF16)],
        compiler_params=pltpu.CompilerParams(dimension_semantics=("parallel", "arbitrary"),
                                             vmem_limit_bytes=vmem + VMEM_HEADROOM),
        name="ffn",
    )(x2d, gpre.reshape(1, D_MODEL), wg_all, wu_all, wd_all, gpost.reshape(1, D_MODEL))


def kernel(x, meta_tokens, norm_mix_pre, w_in, diff_lambda_q1, diff_lambda_k1, diff_lambda_q2, diff_lambda_k2, diff_norm, sb_norm, ssm_a_re, ssm_a_im, ssm_log_dt, ssm_b_re, ssm_b_im, ssm_c_re, ssm_c_im, ssm_d, ssm_w_glu, ssm_b_glu, ssm_norm, ret_norm, w_out, norm_mix_post, norm_ffn_pre, w_ffn_gate, w_ffn_up, w_ffn_down, norm_ffn_post):
    bsz, seq, d_model = x.shape
    assert d_model == D_MODEL and meta_tokens.shape == (N_META, D_MODEL)
    depth = w_in.shape[0]
    n_tok = N_META + seq
    lp = _padded_len(n_tok)
    pad_front = lp - n_tok
    first_real = pad_front + N_META

    meta = jnp.broadcast_to(meta_tokens[None].astype(x.dtype), (bsz, N_META, D_MODEL))
    h = jnp.concatenate([jnp.zeros((bsz, pad_front, D_MODEL), x.dtype), meta, x], axis=1)
    h = h.reshape(bsz * lp, D_MODEL)

    tabs = _rotary_tables(lp, pad_front)
    tri = (jnp.arange(MXU_DIM)[:, None] >= jnp.arange(MXU_DIM)[None, :]).astype(BF16)

    w_in_b, w_out_b = w_in.astype(BF16), w_out.astype(BF16)
    w_gate_b, w_up_b, w_down_b = w_ffn_gate.astype(BF16), w_ffn_up.astype(BF16), w_ffn_down.astype(BF16)

    for l in range(depth):
        lam_init = 0.8 - 0.6 * math.exp(-0.3 * l)
        proj, su_chunks = _in_proj(h, norm_mix_pre[l], w_in_b, l, tabs, lp=lp, pad_front=pad_front)
        lam_vecs = jnp.stack([diff_lambda_q1[l], diff_lambda_k1[l], diff_lambda_q2[l], diff_lambda_k2[l]]).astype(F32)
        o_diff = _diff_attention(proj, lam_vecs, diff_norm[l], bsz=bsz, lp=lp, pad_front=pad_front, lam_init=lam_init)
        o_sb = _sb_attention(proj, tri, sb_norm[l], bsz=bsz, lp=lp, pad_front=pad_front)
        mats = _s5_matrices(ssm_a_re[l], ssm_a_im[l], ssm_log_dt[l], ssm_b_re[l], ssm_b_im[l], ssm_c_re[l], ssm_c_im[l])
        o_ssm = _s5_mixer(proj, su_chunks, mats, ssm_d[l], ssm_w_glu[l], ssm_b_glu[l], ssm_norm[l], bsz=bsz, lp=lp)
        o_ret = _retention(proj, ret_norm[l], bsz=bsz, lp=lp)
        drop = first_real if (l == depth - 1 and first_real % MXU_DIM == 0) else 0
        h = _out_proj((o_diff, o_sb, o_ssm, o_ret), w_out_b, l, h, norm_mix_post[l],
                      rows_per_batch=lp, drop_rows=drop)
        h = _ffn(h, norm_ffn_pre[l], w_gate_b, w_up_b, w_down_b, l, norm_ffn_post[l])

    return h.reshape(bsz, -1, D_MODEL)[:, first_real - drop:]
```

```python
import functools
import math

import jax
import jax.numpy as jnp
from jax import lax
from jax.experimental import pallas as pl
from jax.experimental.pallas import tpu as pltpu

F32 = jnp.float32
BF16 = jnp.bfloat16

D_MODEL = 2048
N_META = 16
GROUP_W = D_MODEL // 4
DIFF_HEADS = 4
DIFF_QK = 64
DIFF_V = GROUP_W // DIFF_HEADS
SB_HEADS = 8
SB_DIM = GROUP_W // SB_HEADS
SSM_GROUP = 16
SSM_NG = GROUP_W // SSM_GROUP
SSM_STATE = 64
RET_HEADS = 4
RET_DIM = GROUP_W // RET_HEADS
FFN_HIDDEN = -(-8 * D_MODEL // (3 * 256)) * 256
IN_COLS = 11 * GROUP_W
ROPE_THETA = 500000.0
ROPE_DIM = DIFF_QK // 4
RET_THETA = 10000.0
NEG_INF = -1e30
EPS = 1e-6
LOG2E = 1.4426950408889634

LANES = 128
MXU_DIM = 256
SEQ_TILE = 768
SSM_CHUNK = 16
DIFF_HEADS_PER_STEP = 2
SB_BLOCKS_PER_STEP = 2
VMEM_LIMIT = 52 * 1024 * 1024
VMEM_HEADROOM = 4 * 1024 * 1024

COL_DQ, COL_DK, COL_DV, COL_SQ, COL_SK, COL_SV, COL_SU, COL_RQ, COL_RK, COL_RV, COL_RG = range(11)


def _padded_len(n_tok):
    lp = -(-n_tok // MXU_DIM) * MXU_DIM
    while lp % SEQ_TILE:
        lp += MXU_DIM
    return lp


def _params(*sem):
    return pltpu.CompilerParams(dimension_semantics=sem, vmem_limit_bytes=VMEM_LIMIT)


def _in_proj_kernel(x_ref, g_ref, w_ref, dc_ref, dsa_ref, dsb_ref, rc_ref, rs_ref, o_ref, su_ref, h_scr, su_scr,
                    *, tm, tiles_per_batch, pad_front):
    i = pl.program_id(0)
    j = pl.program_id(1)

    @pl.when(j == 0)
    def _():
        x = x_ref[...]
        y = x * lax.rsqrt(jnp.mean(x * x, axis=-1, keepdims=True) + EPS) * g_ref[...]
        pos = (i % tiles_per_batch) * tm + lax.broadcasted_iota(jnp.int32, (tm, 1), 0)
        h_scr[...] = jnp.where(pos >= pad_front, y, 0.0).astype(BF16)

    acc = jnp.dot(h_scr[...], w_ref[0], preferred_element_type=F32)
    plain_scale = jnp.where(j == COL_SQ, LOG2E * SB_DIM ** -0.5, 1.0).astype(F32)
    o_ref[...] = (acc * plain_scale).astype(BF16)

    is_diff = (j == COL_DQ) | (j == COL_DK)
    is_ret = (j == COL_RQ) | (j == COL_RK)

    @pl.when(is_diff)
    def _():
        scale = jnp.where(j == COL_DQ, LOG2E * DIFF_QK ** -0.5, 1.0).astype(F32)
        c, sa, sb = dc_ref[...], dsa_ref[...], dsb_ref[...]
        for n in range(GROUP_W // LANES):
            a = acc[:, n * LANES:(n + 1) * LANES]
            r = a * c + pltpu.roll(a, LANES - ROPE_DIM // 2, 1) * sa + pltpu.roll(a, ROPE_DIM // 2, 1) * sb
            o_ref[:, n * LANES:(n + 1) * LANES] = (r * scale).astype(BF16)

    @pl.when(is_ret)
    def _():
        scale = jnp.where(j == COL_RK, RET_DIM ** -0.5, 1.0).astype(F32)
        c, s = rc_ref[...], rs_ref[...]
        for n in range(GROUP_W // LANES):
            a = acc[:, n * LANES:(n + 1) * LANES]
            r = a * c + pltpu.roll(a, RET_DIM // 2, 1) * s
            o_ref[:, n * LANES:(n + 1) * LANES] = (r * scale).astype(BF16)

    @pl.when(j == COL_SU)
    def _():
        for n in range(GROUP_W // LANES):
            su_scr[n] = acc[:, n * LANES:(n + 1) * LANES]
            for s in range(SSM_CHUNK):
                piece = su_scr[n, pl.ds(s, tm // SSM_CHUNK, stride=SSM_CHUNK), :]
                su_ref[n, :, s * LANES:(s + 1) * LANES] = piece.astype(BF16)


def _in_proj(x2d, gain, w_all, layer, tabs, *, lp, pad_front):
    rows = x2d.shape[0]
    tm = SEQ_TILE
    tiles_per_batch = lp // tm
    n_col = IN_COLS // GROUP_W
    tab_spec = pl.BlockSpec((tm, LANES), lambda i, j: (i % tiles_per_batch, 0))
    return pl.pallas_call(
        functools.partial(_in_proj_kernel, tm=tm, tiles_per_batch=tiles_per_batch, pad_front=pad_front),
        grid=(rows // tm, n_col),
        in_specs=[
            pl.BlockSpec((tm, D_MODEL), lambda i, j: (i, 0)),
            pl.BlockSpec((1, D_MODEL), lambda i, j: (0, 0)),
            pl.BlockSpec((1, D_MODEL, GROUP_W), lambda i, j: (layer, 0, j)),
            tab_spec, tab_spec, tab_spec, tab_spec, tab_spec,
        ],
        out_specs=[pl.BlockSpec((tm, GROUP_W), lambda i, j: (i, j)),
                   pl.BlockSpec((GROUP_W // LANES, tm // SSM_CHUNK, SSM_CHUNK * LANES), lambda i, j: (0, i, 0))],
        out_shape=[jax.ShapeDtypeStruct((rows, IN_COLS), BF16),
                   jax.ShapeDtypeStruct((GROUP_W // LANES, rows // SSM_CHUNK, SSM_CHUNK * LANES), BF16)],
        scratch_shapes=[pltpu.VMEM((tm, D_MODEL), BF16), pltpu.VMEM((GROUP_W // LANES, tm, LANES), F32)],
        compiler_params=_params("parallel", "arbitrary"),
        name="in_proj",
    )(x2d, gain.reshape(1, D_MODEL), w_all, *tabs)


def _rotary_tables(lp, pad_front):
    pos = (jnp.arange(lp, dtype=jnp.int32) - pad_front).astype(F32)[:, None]
    lane = jnp.arange(LANES)
    half = ROPE_DIM // 2
    inv = 1.0 / (ROPE_THETA ** (jnp.arange(half, dtype=F32) * (2.0 / ROPE_DIM)))
    ang = pos * inv[None, :]
    d = lane % DIFF_QK
    cos_l = jnp.cos(ang)[:, d % half]
    sin_l = jnp.sin(ang)[:, d % half]
    dc = jnp.where(d < ROPE_DIM, cos_l, 1.0)
    dsa = jnp.where(d < half, -sin_l, 0.0)
    dsb = jnp.where((d >= half) & (d < ROPE_DIM), sin_l, 0.0)
    rhalf = RET_DIM // 2
    rinv = 1.0 / (RET_THETA ** (jnp.arange(rhalf, dtype=F32) * (2.0 / RET_DIM)))
    rang = pos * rinv[None, :]
    r = lane % RET_DIM
    rc = jnp.cos(rang)[:, r % rhalf]
    rs = jnp.where(r < rhalf, -1.0, 1.0) * jnp.sin(rang)[:, r % rhalf]
    return dc, dsa, dsb, rc, rs


def _tile_pairs(nt, descending):
    pairs = [(i, j) for i in range(nt) for j in (range(i, -1, -1) if descending else range(i + 1))]
    return jnp.array(list(zip(*pairs)), dtype=jnp.int32)


def _dispatch_by_mask_kind(i, j, pad_front, step):
    has_pad = pad_front > 0

    @pl.when((j != i) & (j != 0))
    def _():
        step(causal=False, pad=False)

    @pl.when((j == i) & (i != 0))
    def _():
        step(causal=True, pad=False)

    @pl.when((j == 0) & (i != 0))
    def _():
        step(causal=False, pad=has_pad)

    @pl.when(i == 0)
    def _():
        step(causal=True, pad=has_pad)


def _diff_attn_kernel(ij_ref, lam_ref, q_ref, k_ref, v_ref, gn_ref, o_ref, m_scr, l_scr, acc_scr,
                      *, t, pad_front, lam_init):
    pair = pl.program_id(2)
    i = ij_ref[0, pair]
    j = ij_ref[1, pair]

    @pl.when(j == 0)
    def _():
        m_scr[...] = jnp.full(m_scr.shape, NEG_INF, F32)
        l_scr[...] = jnp.zeros(l_scr.shape, F32)
        acc_scr[...] = jnp.zeros(acc_scr.shape, F32)

    def step(causal, pad):
        mask = None
        if causal or pad:
            row = lax.broadcasted_iota(jnp.int32, (t, t), 0)
            col = lax.broadcasted_iota(jnp.int32, (t, t), 1)
            if causal:
                mask = col <= row
            if pad:
                mask = (col >= pad_front) if mask is None else mask & (col >= pad_front)
        lane = lax.broadcasted_iota(jnp.int32, (t, LANES), 1)
        scores, values = [], []
        for hd in range(DIFF_HEADS_PER_STEP):
            lanes = slice(hd * LANES, (hd + 1) * LANES)
            q, k, v = q_ref[:, lanes], k_ref[:, lanes], v_ref[:, lanes]
            values.append(jnp.concatenate([v, jnp.ones_like(v)], axis=1))
            for c in range(2):
                qc = jnp.where((lane >= c * DIFF_QK) & (lane < (c + 1) * DIFF_QK), q, jnp.zeros_like(q))
                s = lax.dot_general(qc, k, (((1,), (1,)), ((), ())), preferred_element_type=F32)
                if mask is not None:
                    s = jnp.where(mask, s, NEG_INF)
                scores.append(s)
        for n, s in enumerate(scores):
            m_prev = m_scr[n]
            m_new = jnp.maximum(m_prev, jnp.max(s, axis=-1, keepdims=True))
            alpha = jnp.exp2(m_prev - m_new)
            p = jnp.exp2(s - jnp.tile(m_new, (1, t // LANES)))
            pv = jnp.dot(p.astype(BF16), values[n // 2], preferred_element_type=F32)
            acc_scr[n] = alpha * acc_scr[n] + pv[:, :DIFF_V]
            l_scr[n] = alpha * l_scr[n] + pv[:, DIFF_V:]
            m_scr[n] = m_new

    _dispatch_by_mask_kind(i, j, pad_front, step)

    @pl.when(j == i)
    def _():
        lv = lam_ref[...]
        lam = (jnp.exp(jnp.sum(lv[0:1] * lv[1:2], axis=-1, keepdims=True))
               - jnp.exp(jnp.sum(lv[2:3] * lv[3:4], axis=-1, keepdims=True)) + lam_init)
        for hd in range(DIFF_HEADS_PER_STEP):
            a, b = 2 * hd, 2 * hd + 1
            o = acc_scr[a] / l_scr[a] - lam * (acc_scr[b] / l_scr[b])
            y = o * lax.rsqrt(jnp.mean(o * o, axis=-1, keepdims=True) + EPS) * gn_ref[...]
            o_ref[:, hd * LANES:(hd + 1) * LANES] = (y * (1.0 - lam_init)).astype(BF16)


def _diff_attention(proj, lam_vecs, gn, *, bsz, lp, pad_front, lam_init):
    t = SEQ_TILE
    nt = lp // t
    ij = _tile_pairs(nt, descending=False)
    hs = DIFF_HEADS_PER_STEP
    w = hs * LANES
    grid_spec = pltpu.PrefetchScalarGridSpec(
        num_scalar_prefetch=1,
        grid=(bsz, DIFF_HEADS // hs, ij.shape[1]),
        in_specs=[
            pl.BlockSpec((4, DIFF_QK), lambda b, h, p, ij: (0, 0)),
            pl.BlockSpec((t, w), lambda b, h, p, ij: (b * nt + ij[0, p], COL_DQ * GROUP_W // w + h)),
            pl.BlockSpec((t, w), lambda b, h, p, ij: (b * nt + ij[1, p], COL_DK * GROUP_W // w + h)),
            pl.BlockSpec((t, w), lambda b, h, p, ij: (b * nt + ij[1, p], COL_DV * GROUP_W // w + h)),
            pl.BlockSpec((1, DIFF_V), lambda b, h, p, ij: (0, 0)),
        ],
        out_specs=pl.BlockSpec((t, w), lambda b, h, p, ij: (b * nt + ij[0, p], h)),
        scratch_shapes=[pltpu.VMEM((2 * hs, t, LANES), F32), pltpu.VMEM((2 * hs, t, LANES), F32),
                        pltpu.VMEM((2 * hs, t, DIFF_V), F32)],
    )
    return pl.pallas_call(
        functools.partial(_diff_attn_kernel, t=t, pad_front=pad_front, lam_init=lam_init),
        grid_spec=grid_spec,
        out_shape=jax.ShapeDtypeStruct((bsz * lp, GROUP_W), BF16),
        compiler_params=_params("parallel", "parallel", "arbitrary"),
        name="diff_attn",
    )(ij, lam_vecs, proj, proj, proj, gn.reshape(1, DIFF_V))


def _sb_attn_kernel(ij_ref, q_ref, k_ref, v_ref, tri_ref, gn_ref, o_ref, run_scr, acc_scr,
                    *, t, sub, pad_front):
    pair = pl.program_id(2)
    i = ij_ref[0, pair]
    j = ij_ref[1, pair]

    @pl.when(j == i)
    def _():
        run_scr[...] = jnp.zeros(run_scr.shape, F32)
        acc_scr[...] = jnp.zeros(acc_scr.shape, F32)

    def block(qh, hh, lanes, r0, c0, mask):
        z = lax.dot_general(qh[r0:], k_ref[c0:c0 + sub, lanes], (((1,), (1,)), ((), ())),
                            preferred_element_type=F32)
        sign_bit = jnp.uint32(0x80000000)
        neg_abs = lax.bitcast_convert_type(lax.bitcast_convert_type(z, jnp.uint32) | sign_bit, F32)
        sp = jnp.maximum(z, 0.0) + jnp.log(1.0 + jnp.exp2(neg_abs)) * LOG2E
        if mask is not None:
            sp = jnp.where(mask, sp, 0.0)
        tail = jnp.dot(sp.astype(BF16), tri_ref[...], preferred_element_type=F32)
        run = run_scr[hh, r0:, :]
        log_w = z - tail - jnp.tile(run, (1, sub // LANES))
        if mask is not None:
            log_w = jnp.where(mask, log_w, NEG_INF)
        w = jnp.exp2(log_w)
        acc_scr[hh, r0:, :] += jnp.dot(w.astype(BF16), v_ref[c0:c0 + sub, lanes], preferred_element_type=F32)
        run_scr[hh, r0:, :] = run + jnp.sum(sp, axis=-1, keepdims=True)

    heads_per_block = LANES // SB_DIM

    def step(causal, pad):
        lane = lax.broadcasted_iota(jnp.int32, (t, LANES), 1)
        first_key = pad_front if pad else 0
        for blk in range(SB_BLOCKS_PER_STEP):
            lanes = slice(blk * LANES, (blk + 1) * LANES)
            q = q_ref[:, lanes]
            for hh in range(heads_per_block):
                qh = jnp.where((lane >= hh * SB_DIM) & (lane < (hh + 1) * SB_DIM), q, jnp.zeros_like(q))
                for c0 in reversed(range(0, t, sub)):
                    if c0 + sub <= first_key:
                        continue
                    r0 = c0 if causal else 0
                    mask = None
                    if causal or c0 < first_key:
                        row = r0 + lax.broadcasted_iota(jnp.int32, (t - r0, sub), 0)
                        col = c0 + lax.broadcasted_iota(jnp.int32, (t - r0, sub), 1)
                        if causal:
                            mask = col < row
                        if c0 < first_key:
                            mask = (col >= first_key) if mask is None else mask & (col >= first_key)
                    block(qh, blk * heads_per_block + hh, lanes, r0, c0, mask)

    _dispatch_by_mask_kind(i, j, pad_front, step)

    @pl.when(j == 0)
    def _():
        lane = lax.broadcasted_iota(jnp.int32, (t, LANES), 1)
        for blk in range(SB_BLOCKS_PER_STEP):
            y = jnp.zeros((t, LANES), F32)
            for hh in range(heads_per_block):
                in_head = (lane >= hh * SB_DIM) & (lane < (hh + 1) * SB_DIM)
                o = jnp.where(in_head, acc_scr[blk * heads_per_block + hh], 0.0)
                ms = jnp.sum(o * o, axis=-1, keepdims=True) * (1.0 / SB_DIM)
                y = y + o * lax.rsqrt(ms + EPS)
            o_ref[:, blk * LANES:(blk + 1) * LANES] = (y * gn_ref[...]).astype(BF16)


def _sb_attention(proj, tri, gn, *, bsz, lp, pad_front):
    t = SEQ_TILE
    nt = lp // t
    w = SB_BLOCKS_PER_STEP * LANES
    n_slot = SB_BLOCKS_PER_STEP * (LANES // SB_DIM)
    ij = _tile_pairs(nt, descending=True)
    grid_spec = pltpu.PrefetchScalarGridSpec(
        num_scalar_prefetch=1,
        grid=(bsz, GROUP_W // w, ij.shape[1]),
        in_specs=[
            pl.BlockSpec((t, w), lambda b, h, p, ij: (b * nt + ij[0, p], COL_SQ * GROUP_W // w + h)),
            pl.BlockSpec((t, w), lambda b, h, p, ij: (b * nt + ij[1, p], COL_SK * GROUP_W // w + h)),
            pl.BlockSpec((t, w), lambda b, h, p, ij: (b * nt + ij[1, p], COL_SV * GROUP_W // w + h)),
            pl.BlockSpec((MXU_DIM, MXU_DIM), lambda b, h, p, ij: (0, 0)),
            pl.BlockSpec((1, LANES), lambda b, h, p, ij: (0, 0)),
        ],
        out_specs=pl.BlockSpec((t, w), lambda b, h, p, ij: (b * nt + ij[0, p], h)),
        scratch_shapes=[pltpu.VMEM((n_slot, t, LANES), F32), pltpu.VMEM((n_slot, t, LANES), F32)],
    )
    return pl.pallas_call(
        functools.partial(_sb_attn_kernel, t=t, sub=MXU_DIM, pad_front=pad_front),
        grid_spec=grid_spec,
        out_shape=jax.ShapeDtypeStruct((bsz * lp, GROUP_W), BF16),
        compiler_params=_params("parallel", "parallel", "arbitrary"),
        name="sb_attn",
    )(ij, proj, proj, proj, tri, jnp.tile(gn.reshape(1, SB_DIM), (1, LANES // SB_DIM)))


def _s5_chunk_kernel(u_ref, d_ref, e_ref, y_ref, s_ref):
    _s5_intra(u_ref, d_ref, y_ref)
    _s5_state(u_ref, e_ref, s_ref)


def _s5_intra(u_ref, d_ref, y_ref):
    zero = jnp.zeros((LANES, LANES), BF16)
    for t0 in range(0, SSM_CHUNK, 2):
        w = jnp.concatenate(
            [jnp.concatenate([d_ref[0, t0 - s] if s <= t0 else zero, d_ref[0, t0 + 1 - s]], axis=1)
             for s in range(t0 + 2)], axis=0)
        y_ref[0, :, t0 * LANES:(t0 + 2) * LANES] = jnp.dot(
            u_ref[0, :, :(t0 + 2) * LANES], w, preferred_element_type=F32)


def _s5_state(u_ref, e_ref, o_ref):
    u = u_ref[0]
    cols = 4 * SSM_STATE
    row = lax.broadcasted_iota(jnp.int32, (LANES, cols), 0)
    for g in range(LANES // SSM_GROUP):
        in_group = (row >= g * SSM_GROUP) & (row < (g + 1) * SSM_GROUP)
        w = jnp.concatenate([jnp.where(in_group, e_ref[0, s], jnp.zeros((LANES, cols), BF16))
                             for s in range(SSM_CHUNK)], axis=0)
        o_ref[:, g * cols:(g + 1) * cols] = jnp.dot(u, w, preferred_element_type=F32)


def _s5_scan_kernel(s_ref, a1_ref, a2_ref, x_ref, *, n_chunk, bsz, groups):
    a1 = a1_ref[...]
    a2 = a2_ref[...]
    w = groups * LANES

    def body(c, carry):
        new = []
        for b in range(bsz):
            x, xs = carry[b]
            row = s_ref[pl.ds(b * n_chunk + c, 1), :]
            x_ref[pl.ds(b * n_chunk + c, 1), :] = x
            s_x = jnp.concatenate([row[:, g * 2 * LANES:g * 2 * LANES + LANES] for g in range(groups)], axis=1)
            s_xs = jnp.concatenate([row[:, g * 2 * LANES + LANES:(g + 1) * 2 * LANES] for g in range(groups)], axis=1)
            new.append((a1 * x + a2 * xs + s_x, a1 * xs - a2 * x + s_xs))
        return tuple(new)

    zero = jnp.zeros((1, w), F32)
    lax.fori_loop(0, n_chunk, body, tuple((zero, zero) for _ in range(bsz)))


def _s5_cross_kernel(yi_ref, x_ref, c_ref, y_ref):
    x = x_ref[...].astype(BF16)
    for t0 in range(0, SSM_CHUNK, 2):
        w = jnp.concatenate([c_ref[0, t0], c_ref[0, t0 + 1]], axis=1)
        cols = slice(t0 * LANES, (t0 + 2) * LANES)
        y_ref[0, :, cols] = yi_ref[0, :, cols] + jnp.dot(x, w, preferred_element_type=F32)


def _s5_out_kernel(y_ref, u_ref, d_ref, w_ref, b_ref, gn_ref, o_ref, y_scr):
    n_chunk = y_ref.shape[1]
    n_blk = GROUP_W // LANES
    for n in range(n_blk):
        for s in range(SSM_CHUNK):
            y_scr[n, pl.ds(s, n_chunk, stride=SSM_CHUNK), :] = y_ref[n, :, s * LANES:(s + 1) * LANES]
    y = jnp.concatenate([y_scr[n] for n in range(n_blk)], axis=1) + d_ref[...] * u_ref[...].astype(F32)
    g = 0.5 * y * (1.0 + jnp.tanh(math.sqrt(2.0 / math.pi) * (y + 0.044715 * (y * y * y))))
    gate = jnp.dot(g.astype(BF16), w_ref[...], preferred_element_type=F32) + b_ref[...]
    o = g * (1.0 / (1.0 + jnp.exp(-gate)))
    o_ref[...] = (o * lax.rsqrt(jnp.mean(o * o, axis=-1, keepdims=True) + EPS) * gn_ref[...]).astype(BF16)


def _s5_matrices(a_re, a_im, log_dt, b_re, b_im, c_re, c_im):
    hp = lax.Precision.HIGHEST
    t = SSM_CHUNK
    dt = jnp.exp(log_dt.astype(F32))[:, None]
    ar, ai = a_re.astype(F32), a_im.astype(F32)
    dpow = jnp.arange(t + 1, dtype=F32)[:, None, None]
    mag = jnp.exp(dpow * (ar * dt)[None])
    pr = mag * jnp.cos(dpow * (ai * dt)[None])
    pi = mag * jnp.sin(dpow * (ai * dt)[None])
    nr, ni = pr[1] - 1.0, pi[1]
    den = ar * ar + ai * ai
    fr = (nr * ar + ni * ai) / den
    fi = (ni * ar - nr * ai) / den
    br, bi = b_re.astype(F32), b_im.astype(F32)
    bbr = fr[..., None] * br - fi[..., None] * bi
    bbi = fr[..., None] * bi + fi[..., None] * br
    cr, ci = c_re.astype(F32), c_im.astype(F32)
    car = cr[None] * pr[:, :, None, :] - ci[None] * pi[:, :, None, :]
    cai = cr[None] * pi[:, :, None, :] + ci[None] * pr[:, :, None, :]
    kern = jnp.einsum('dghp,gpk->dghk', jnp.concatenate([car[:t], -cai[:t]], axis=-1),
                      jnp.concatenate([bbr, bbi], axis=1), precision=hp)
    gl = LANES // SSM_GROUP
    nb = SSM_NG // gl
    place = (jnp.arange(LANES)[None, None, :]
             == jnp.arange(gl)[:, None, None] * SSM_GROUP + jnp.arange(SSM_GROUP)[None, :, None]).astype(F32)
    kern_b = kern.reshape(t, nb, gl, SSM_GROUP, SSM_GROUP)
    d_blocks = jnp.einsum('dbghk,ghl->bdgkl', kern_b, place,
                          preferred_element_type=BF16).reshape(nb, t, LANES, LANES)
    rpow = (t - 1.0) - jnp.arange(t, dtype=F32)[:, None, None]
    rmag = jnp.exp(rpow * (ar * dt)[None])
    rev_r = (rmag * jnp.cos(rpow * (ai * dt)[None]))[:, :, None, :]
    rev_i = (rmag * jnp.sin(rpow * (ai * dt)[None]))[:, :, None, :]
    bbr_t, bbi_t = jnp.swapaxes(bbr, 1, 2)[None], jnp.swapaxes(bbi, 1, 2)[None]
    sre = rev_r * bbr_t - rev_i * bbi_t
    sim = rev_r * bbi_t + rev_i * bbr_t
    st = jnp.concatenate([sre, sim, sim, sre], axis=-1)
    e_blocks = jnp.transpose(st.astype(BF16).reshape(t, nb, LANES, 4 * SSM_STATE), (1, 0, 2, 3))
    cx = jnp.concatenate([car[1:], -cai[1:]], axis=-1)
    cx = cx.reshape(t, nb, gl, SSM_GROUP, 2 * SSM_STATE)
    c_blocks = jnp.einsum('tbghc,ghl->btgcl', cx, place,
                          preferred_element_type=BF16).reshape(nb, t, gl * 2 * SSM_STATE, LANES)
    a1 = jnp.concatenate([pr[t], pr[t]], axis=-1).reshape(1, SSM_NG * LANES)
    a2 = jnp.concatenate([-pi[t], pi[t]], axis=-1).reshape(1, SSM_NG * LANES)
    return d_blocks.astype(BF16), e_blocks.astype(BF16), c_blocks.astype(BF16), a1, a2


def _s5_mixer(proj, u_c, mats_all, layer, d, w_glu, b_glu, gn, *, bsz, lp):
    t = SSM_CHUNK
    nb = GROUP_W // LANES
    d_blocks, e_blocks, c_blocks = (m.reshape((-1,) + m.shape[2:]) for m in mats_all[:3])
    a1, a2 = mats_all[3][layer], mats_all[4][layer]
    first = layer * nb
    gl = LANES // SSM_GROUP
    cl = t * LANES
    sl = gl * 4 * SSM_STATE
    xl = gl * 2 * SSM_STATE
    n_chunk = lp // t
    nc = bsz * n_chunk
    rows = bsz * lp
    rt = nc // 2

    y_intra, s_end = pl.pallas_call(
        _s5_chunk_kernel,
        grid=(nb, nc // rt),
        in_specs=[
            pl.BlockSpec((1, rt, cl), lambda n, r: (n, r, 0)),
            pl.BlockSpec((1, t, LANES, LANES), lambda n, r: (first + n, 0, 0, 0)),
            pl.BlockSpec((1, t, LANES, 4 * SSM_STATE), lambda n, r: (first + n, 0, 0, 0)),
        ],
        out_specs=[pl.BlockSpec((1, rt, cl), lambda n, r: (n, r, 0)),
                   pl.BlockSpec((rt, sl), lambda n, r: (r, n))],
        out_shape=[jax.ShapeDtypeStruct((nb, nc, cl), F32),
                   jax.ShapeDtypeStruct((nc, nb * sl), F32)],
        compiler_params=_params("parallel", "parallel"),
        name="s5_chunk",
    )(u_c, d_blocks, e_blocks)

    x_start = pl.pallas_call(
        functools.partial(_s5_scan_kernel, n_chunk=n_chunk, bsz=bsz, groups=gl),
        grid=(nb,),
        in_specs=[
            pl.BlockSpec((nc, sl), lambda n: (0, n)),
            pl.BlockSpec((1, xl), lambda n: (0, n)),
            pl.BlockSpec((1, xl), lambda n: (0, n)),
        ],
        out_specs=pl.BlockSpec((nc, xl), lambda n: (0, n)),
        out_shape=jax.ShapeDtypeStruct((nc, nb * xl), F32),
        compiler_params=_params("parallel"),
        name="s5_scan",
    )(s_end, a1, a2)

    y_c = pl.pallas_call(
        _s5_cross_kernel,
        grid=(nb, nc // rt),
        in_specs=[
            pl.BlockSpec((1, rt, cl), lambda n, r: (n, r, 0)),
            pl.BlockSpec((rt, xl), lambda n, r: (r, n)),
            pl.BlockSpec((1, t, xl, LANES), lambda n, r: (first + n, 0, 0, 0)),
        ],
        out_specs=pl.BlockSpec((1, rt, cl), lambda n, r: (n, r, 0)),
        out_shape=jax.ShapeDtypeStruct((nb, nc, cl), F32),
        compiler_params=_params("parallel", "parallel"),
        name="s5_cross",
    )(y_intra, x_start, c_blocks)

    tm = SEQ_TILE
    vec = lambda: pl.BlockSpec((1, GROUP_W), lambda i: (0, 0))
    return pl.pallas_call(
        _s5_out_kernel,
        grid=(rows // tm,),
        in_specs=[
            pl.BlockSpec((nb, tm // t, cl), lambda i: (0, i, 0)),
            pl.BlockSpec((tm, GROUP_W), lambda i: (i, COL_SU)),
            vec(),
            pl.BlockSpec((GROUP_W, GROUP_W), lambda i: (0, 0)),
            vec(), vec(),
        ],
        out_specs=pl.BlockSpec((tm, GROUP_W), lambda i: (i, 0)),
        out_shape=jax.ShapeDtypeStruct((rows, GROUP_W), BF16),
        scratch_shapes=[pltpu.VMEM((nb, tm, LANES), F32)],
        compiler_params=_params("parallel"),
        name="s5_out",
    )(y_c, proj, d.reshape(1, GROUP_W), w_glu.astype(BF16), b_glu.reshape(1, GROUP_W), gn.reshape(1, GROUP_W))


def _retention_kernel(lg_ref, q_ref, k_ref, v_ref, g_ref, gn_ref, o_ref, st_scr, dmat_scr, *, t):
    c = pl.program_id(2)
    lg = lg_ref[0][:, :1]

    @pl.when(c == 0)
    def _():
        st_scr[...] = jnp.zeros(st_scr.shape, F32)
        ti = lax.broadcasted_iota(jnp.int32, (t, t), 0)
        si = lax.broadcasted_iota(jnp.int32, (t, t), 1)
        rel = (ti - si).astype(F32)
        dmat_scr[...] = jnp.where(ti >= si, jnp.exp(lg * jnp.maximum(rel, 0.0)), 0.0)

    q, k, v = q_ref[...], k_ref[...], v_ref[...]
    col = lax.broadcasted_iota(jnp.int32, (t, 1), 0).astype(F32)
    q_decay = jnp.exp(lg * (col + 1.0))
    k_decay = jnp.exp(lg * (t - 1.0 - col))
    chunk_decay = jnp.exp(lg * float(t))

    s = lax.dot_general(q, k, (((1,), (1,)), ((), ())), preferred_element_type=F32) * dmat_scr[...]
    state = st_scr[...]
    o = (jnp.dot(s.astype(BF16), v, preferred_element_type=F32)
         + jnp.dot(q, state.astype(BF16), preferred_element_type=F32) * q_decay)
    kd = (k.astype(F32) * k_decay).astype(BF16)
    st_scr[...] = chunk_decay * state + lax.dot_general(kd, v, (((0,), (0,)), ((), ())),
                                                        preferred_element_type=F32)

    mu = jnp.mean(o, axis=-1, keepdims=True)
    oc = o - mu
    y = oc * lax.rsqrt(jnp.mean(oc * oc, axis=-1, keepdims=True) + EPS) * gn_ref[...]
    gate = g_ref[...].astype(F32)
    o_ref[...] = (y * (gate / (1.0 + jnp.exp(-gate)))).astype(BF16)


def _retention(proj, gn, *, bsz, lp):
    t = SEQ_TILE
    nt = lp // t
    log_gamma = jnp.log1p(-jnp.exp2(-5.0 - jnp.arange(RET_HEADS, dtype=F32)))
    lg = jnp.broadcast_to(log_gamma[:, None, None], (RET_HEADS, 1, LANES))
    blk = lambda off: pl.BlockSpec((t, LANES), lambda b, h, c: (b * nt + c, off * 4 + h))
    return pl.pallas_call(
        functools.partial(_retention_kernel, t=t),
        grid=(bsz, RET_HEADS, nt),
        in_specs=[
            pl.BlockSpec((1, 1, LANES), lambda b, h, c: (h, 0, 0)),
            blk(COL_RQ), blk(COL_RK), blk(COL_RV), blk(COL_RG),
            pl.BlockSpec((1, RET_DIM), lambda b, h, c: (0, 0)),
        ],
        out_specs=pl.BlockSpec((t, LANES), lambda b, h, c: (b * nt + c, h)),
        out_shape=jax.ShapeDtypeStruct((bsz * lp, GROUP_W), BF16),
        scratch_shapes=[pltpu.VMEM((RET_DIM, RET_DIM), F32), pltpu.VMEM((t, t), F32)],
        compiler_params=_params("parallel", "parallel", "arbitrary"),
        name="retention",
    )(lg, proj, proj, proj, proj, gn.reshape(1, RET_DIM))


def _out_proj_kernel(a_ref, b_ref, c_ref, d_ref, w_ref, x_ref, g_ref, o_ref):
    acc = jnp.dot(a_ref[...], w_ref[0], preferred_element_type=F32)
    acc += jnp.dot(b_ref[...], w_ref[1], preferred_element_type=F32)
    acc += jnp.dot(c_ref[...], w_ref[2], preferred_element_type=F32)
    acc += jnp.dot(d_ref[...], w_ref[3], preferred_element_type=F32)
    y = acc * lax.rsqrt(jnp.mean(acc * acc, axis=-1, keepdims=True) + EPS) * g_ref[...]
    o_ref[...] = x_ref[...] + y


def _out_proj(mixes, w_all, layer, x2d, gain, *, rows_per_batch, drop_rows=0):
    rows = x2d.shape[0]
    tm = MXU_DIM
    assert drop_rows % tm == 0 and rows_per_batch % tm == 0
    per_batch, skip = rows_per_batch // tm, drop_rows // tm

    def out_idx(i):
        return ((i // per_batch) * (per_batch - skip) + jnp.maximum(i % per_batch - skip, 0), 0)

    mix_spec = pl.BlockSpec((tm, GROUP_W), lambda i: (i, 0))
    return pl.pallas_call(
        _out_proj_kernel,
        grid=(rows // tm,),
        in_specs=[
            mix_spec, mix_spec, mix_spec, mix_spec,
            pl.BlockSpec((4, GROUP_W, D_MODEL), lambda i: (layer, 0, 0)),
            pl.BlockSpec((tm, D_MODEL), lambda i: (i, 0)),
            pl.BlockSpec((1, D_MODEL), lambda i: (0, 0)),
        ],
        out_specs=pl.BlockSpec((tm, D_MODEL), out_idx),
        out_shape=jax.ShapeDtypeStruct((rows // per_batch * (per_batch - skip), D_MODEL), F32),
        compiler_params=_params("arbitrary"),
        name="out_proj",
    )(*mixes, w_all.reshape(-1, GROUP_W, D_MODEL), x2d, gain.reshape(1, D_MODEL))


def _ffn_kernel(x_ref, gpre_ref, wg_ref, wu_ref, wd_ref, gpost_ref, o_ref, h_scr):
    k = pl.program_id(1)

    @pl.when(k == 0)
    def _():
        x = x_ref[...]
        h_scr[...] = (x * lax.rsqrt(jnp.mean(x * x, axis=-1, keepdims=True) + EPS) * gpre_ref[...]).astype(BF16)

    h = h_scr[...]
    gate = jnp.dot(h, wg_ref[0], preferred_element_type=F32)
    up = jnp.dot(h, wu_ref[0], preferred_element_type=F32)
    act = (gate / (1.0 + jnp.exp(-gate)) * up).astype(BF16)
    part = jnp.dot(act, wd_ref[0], preferred_element_type=F32)

    @pl.when(k == 0)
    def _():
        o_ref[...] = part

    @pl.when(k > 0)
    def _():
        o_ref[...] += part

    @pl.when(k == pl.num_programs(1) - 1)
    def _():
        f = o_ref[...]
        o_ref[...] = x_ref[...] + f * lax.rsqrt(jnp.mean(f * f, axis=-1, keepdims=True) + EPS) * gpost_ref[...]


def _ffn(x2d, gpre, wg_all, wu_all, wd_all, layer, gpost):
    rows = x2d.shape[0]
    tm = next(m for m in (SEQ_TILE, 2 * MXU_DIM, MXU_DIM) if rows % m == 0)
    tf = GROUP_W
    vmem = (2 * 2 * tm * D_MODEL * 4 + tm * D_MODEL * 2 + 2 * 3 * D_MODEL * tf * 2
            + 2 * tm * tf * 4 + tm * tf * 2 + tm * D_MODEL * 4)
    return pl.pallas_call(
        _ffn_kernel,
        grid=(rows // tm, FFN_HIDDEN // tf),
        in_specs=[
            pl.BlockSpec((tm, D_MODEL), lambda i, k: (i, 0)),
            pl.BlockSpec((1, D_MODEL), lambda i, k: (0, 0)),
            pl.BlockSpec((1, D_MODEL, tf), lambda i, k: (layer, 0, k)),
            pl.BlockSpec((1, D_MODEL, tf), lambda i, k: (layer, 0, k)),
            pl.BlockSpec((1, tf, D_MODEL), lambda i, k: (layer, k, 0)),
            pl.BlockSpec((1, D_MODEL), lambda i, k: (0, 0)),
        ],
        out_specs=pl.BlockSpec((tm, D_MODEL), lambda i, k: (i, 0)),
        out_shape=jax.ShapeDtypeStruct((rows, D_MODEL), F32),
        scratch_shapes=[pltpu.VMEM((tm, D_MODEL), BF16)],
        compiler_params=pltpu.CompilerParams(dimension_semantics=("parallel", "arbitrary"),
                                             vmem_limit_bytes=vmem + VMEM_HEADROOM),
        name="ffn",
    )(x2d, gpre.reshape(1, D_MODEL), wg_all, wu_all, wd_all, gpost.reshape(1, D_MODEL))


def kernel(x, meta_tokens, norm_mix_pre, w_in, diff_lambda_q1, diff_lambda_k1, diff_lambda_q2, diff_lambda_k2, diff_norm, sb_norm, ssm_a_re, ssm_a_im, ssm_log_dt, ssm_b_re, ssm_b_im, ssm_c_re, ssm_c_im, ssm_d, ssm_w_glu, ssm_b_glu, ssm_norm, ret_norm, w_out, norm_mix_post, norm_ffn_pre, w_ffn_gate, w_ffn_up, w_ffn_down, norm_ffn_post):
    bsz, seq, d_model = x.shape
    assert d_model == D_MODEL and meta_tokens.shape == (N_META, D_MODEL)
    depth = w_in.shape[0]
    n_tok = N_META + seq
    lp = _padded_len(n_tok)
    pad_front = lp - n_tok
    first_real = pad_front + N_META

    meta = jnp.broadcast_to(meta_tokens[None].astype(x.dtype), (bsz, N_META, D_MODEL))
    h = jnp.concatenate([jnp.zeros((bsz, pad_front, D_MODEL), x.dtype), meta, x], axis=1)
    h = h.reshape(bsz * lp, D_MODEL)

    tabs = _rotary_tables(lp, pad_front)
    tri = (jnp.arange(MXU_DIM)[:, None] >= jnp.arange(MXU_DIM)[None, :]).astype(BF16)

    w_in_b, w_out_b = w_in.astype(BF16), w_out.astype(BF16)
    w_gate_b, w_up_b, w_down_b = w_ffn_gate.astype(BF16), w_ffn_up.astype(BF16), w_ffn_down.astype(BF16)

    s5_mats = jax.vmap(_s5_matrices)(ssm_a_re, ssm_a_im, ssm_log_dt, ssm_b_re, ssm_b_im, ssm_c_re, ssm_c_im)

    for l in range(depth):
        lam_init = 0.8 - 0.6 * math.exp(-0.3 * l)
        proj, su_chunks = _in_proj(h, norm_mix_pre[l], w_in_b, l, tabs, lp=lp, pad_front=pad_front)
        lam_vecs = jnp.stack([diff_lambda_q1[l], diff_lambda_k1[l], diff_lambda_q2[l], diff_lambda_k2[l]]).astype(F32)
        o_diff = _diff_attention(proj, lam_vecs, diff_norm[l], bsz=bsz, lp=lp, pad_front=pad_front, lam_init=lam_init)
        o_sb = _sb_attention(proj, tri, sb_norm[l], bsz=bsz, lp=lp, pad_front=pad_front)
        o_ssm = _s5_mixer(proj, su_chunks, s5_mats, l, ssm_d[l], ssm_w_glu[l], ssm_b_glu[l], ssm_norm[l], bsz=bsz, lp=lp)
        o_ret = _retention(proj, ret_norm[l], bsz=bsz, lp=lp)
        drop = first_real if (l == depth - 1 and first_real % MXU_DIM == 0) else 0
        h = _out_proj((o_diff, o_sb, o_ssm, o_ret), w_out_b, l, h, norm_mix_post[l],
                      rows_per_batch=lp, drop_rows=drop)
        h = _ffn(h, norm_ffn_pre[l], w_gate_b, w_up_b, w_down_b, l, norm_ffn_post[l])

    return h.reshape(bsz, -1, D_MODEL)[:, first_real - drop:]
```

```python
import functools
import math

import jax
import jax.numpy as jnp
from jax import lax
from jax.experimental import pallas as pl
from jax.experimental.pallas import tpu as pltpu

F32 = jnp.float32
BF16 = jnp.bfloat16

D_MODEL = 2048
N_META = 16
GROUP_W = D_MODEL // 4
DIFF_HEADS = 4
DIFF_QK = 64
DIFF_V = GROUP_W // DIFF_HEADS
SB_HEADS = 8
SB_DIM = GROUP_W // SB_HEADS
SSM_GROUP = 16
SSM_NG = GROUP_W // SSM_GROUP
SSM_STATE = 64
RET_HEADS = 4
RET_DIM = GROUP_W // RET_HEADS
FFN_HIDDEN = -(-8 * D_MODEL // (3 * 256)) * 256
IN_COLS = 11 * GROUP_W
ROPE_THETA = 500000.0
ROPE_DIM = DIFF_QK // 4
RET_THETA = 10000.0
NEG_INF = -1e30
EPS = 1e-6
LOG2E = 1.4426950408889634

LANES = 128
MXU_DIM = 256
SEQ_TILE = 768
SSM_CHUNK = 16
DIFF_HEADS_PER_STEP = 2
SB_BLOCKS_PER_STEP = 4
VMEM_LIMIT = 52 * 1024 * 1024
VMEM_HEADROOM = 4 * 1024 * 1024

COL_DQ, COL_DK, COL_DV, COL_SQ, COL_SK, COL_SV, COL_SU, COL_RQ, COL_RK, COL_RV, COL_RG = range(11)


def _padded_len(n_tok):
    lp = -(-n_tok // MXU_DIM) * MXU_DIM
    while lp % SEQ_TILE:
        lp += MXU_DIM
    return lp


def _params(*sem):
    return pltpu.CompilerParams(dimension_semantics=sem, vmem_limit_bytes=VMEM_LIMIT)


def _in_proj_kernel(x_ref, g_ref, w_ref, dc_ref, dsa_ref, dsb_ref, rc_ref, rs_ref, o_ref, su_ref, h_scr, su_scr,
                    *, tm, tiles_per_batch, pad_front):
    i = pl.program_id(0)
    j = pl.program_id(1)

    @pl.when(j == 0)
    def _():
        x = x_ref[...]
        y = x * lax.rsqrt(jnp.mean(x * x, axis=-1, keepdims=True) + EPS) * g_ref[...]
        pos = (i % tiles_per_batch) * tm + lax.broadcasted_iota(jnp.int32, (tm, 1), 0)
        h_scr[...] = jnp.where(pos >= pad_front, y, 0.0).astype(BF16)

    acc = jnp.dot(h_scr[...], w_ref[0], preferred_element_type=F32)
    plain_scale = jnp.where(j == COL_SQ, LOG2E * SB_DIM ** -0.5, 1.0).astype(F32)
    o_ref[...] = (acc * plain_scale).astype(BF16)

    is_diff = (j == COL_DQ) | (j == COL_DK)
    is_ret = (j == COL_RQ) | (j == COL_RK)

    @pl.when(is_diff)
    def _():
        scale = jnp.where(j == COL_DQ, LOG2E * DIFF_QK ** -0.5, 1.0).astype(F32)
        c, sa, sb = dc_ref[...], dsa_ref[...], dsb_ref[...]
        for n in range(GROUP_W // LANES):
            a = acc[:, n * LANES:(n + 1) * LANES]
            r = a * c + pltpu.roll(a, LANES - ROPE_DIM // 2, 1) * sa + pltpu.roll(a, ROPE_DIM // 2, 1) * sb
            o_ref[:, n * LANES:(n + 1) * LANES] = (r * scale).astype(BF16)

    @pl.when(is_ret)
    def _():
        scale = jnp.where(j == COL_RK, RET_DIM ** -0.5, 1.0).astype(F32)
        c, s = rc_ref[...], rs_ref[...]
        for n in range(GROUP_W // LANES):
            a = acc[:, n * LANES:(n + 1) * LANES]
            r = a * c + pltpu.roll(a, RET_DIM // 2, 1) * s
            o_ref[:, n * LANES:(n + 1) * LANES] = (r * scale).astype(BF16)

    @pl.when(j == COL_SU)
    def _():
        for n in range(GROUP_W // LANES):
            su_scr[n] = acc[:, n * LANES:(n + 1) * LANES]
            for s in range(SSM_CHUNK):
                piece = su_scr[n, pl.ds(s, tm // SSM_CHUNK, stride=SSM_CHUNK), :]
                su_ref[n, :, s * LANES:(s + 1) * LANES] = piece.astype(BF16)


def _in_proj(x2d, gain, w_all, layer, tabs, *, lp, pad_front):
    rows = x2d.shape[0]
    tm = SEQ_TILE
    tiles_per_batch = lp // tm
    n_col = IN_COLS // GROUP_W
    tab_spec = pl.BlockSpec((tm, LANES), lambda i, j: (i % tiles_per_batch, 0))
    return pl.pallas_call(
        functools.partial(_in_proj_kernel, tm=tm, tiles_per_batch=tiles_per_batch, pad_front=pad_front),
        grid=(rows // tm, n_col),
        in_specs=[
            pl.BlockSpec((tm, D_MODEL), lambda i, j: (i, 0)),
            pl.BlockSpec((1, D_MODEL), lambda i, j: (0, 0)),
            pl.BlockSpec((1, D_MODEL, GROUP_W), lambda i, j: (layer, 0, j)),
            tab_spec, tab_spec, tab_spec, tab_spec, tab_spec,
        ],
        out_specs=[pl.BlockSpec((tm, GROUP_W), lambda i, j: (i, j)),
                   pl.BlockSpec((GROUP_W // LANES, tm // SSM_CHUNK, SSM_CHUNK * LANES), lambda i, j: (0, i, 0))],
        out_shape=[jax.ShapeDtypeStruct((rows, IN_COLS), BF16),
                   jax.ShapeDtypeStruct((GROUP_W // LANES, rows // SSM_CHUNK, SSM_CHUNK * LANES), BF16)],
        scratch_shapes=[pltpu.VMEM((tm, D_MODEL), BF16), pltpu.VMEM((GROUP_W // LANES, tm, LANES), F32)],
        compiler_params=_params("parallel", "arbitrary"),
        name="in_proj",
    )(x2d, gain.reshape(1, D_MODEL), w_all, *tabs)


def _rotary_tables(lp, pad_front):
    pos = (jnp.arange(lp, dtype=jnp.int32) - pad_front).astype(F32)[:, None]
    lane = jnp.arange(LANES)
    half = ROPE_DIM // 2
    inv = 1.0 / (ROPE_THETA ** (jnp.arange(half, dtype=F32) * (2.0 / ROPE_DIM)))
    ang = pos * inv[None, :]
    d = lane % DIFF_QK
    cos_l = jnp.cos(ang)[:, d % half]
    sin_l = jnp.sin(ang)[:, d % half]
    dc = jnp.where(d < ROPE_DIM, cos_l, 1.0)
    dsa = jnp.where(d < half, -sin_l, 0.0)
    dsb = jnp.where((d >= half) & (d < ROPE_DIM), sin_l, 0.0)
    rhalf = RET_DIM // 2
    rinv = 1.0 / (RET_THETA ** (jnp.arange(rhalf, dtype=F32) * (2.0 / RET_DIM)))
    rang = pos * rinv[None, :]
    r = lane % RET_DIM
    rc = jnp.cos(rang)[:, r % rhalf]
    rs = jnp.where(r < rhalf, -1.0, 1.0) * jnp.sin(rang)[:, r % rhalf]
    return dc, dsa, dsb, rc, rs


def _tile_pairs(nt, descending):
    pairs = [(i, j) for i in range(nt) for j in (range(i, -1, -1) if descending else range(i + 1))]
    return jnp.array(list(zip(*pairs)), dtype=jnp.int32)


def _dispatch_by_mask_kind(i, j, pad_front, step):
    has_pad = pad_front > 0

    @pl.when((j != i) & (j != 0))
    def _():
        step(causal=False, pad=False)

    @pl.when((j == i) & (i != 0))
    def _():
        step(causal=True, pad=False)

    @pl.when((j == 0) & (i != 0))
    def _():
        step(causal=False, pad=has_pad)

    @pl.when(i == 0)
    def _():
        step(causal=True, pad=has_pad)


def _diff_attn_kernel(ij_ref, lam_ref, q_ref, k_ref, v_ref, gn_ref, o_ref, m_scr, l_scr, acc_scr,
                      *, t, pad_front, lam_init):
    pair = pl.program_id(2)
    i = ij_ref[0, pair]
    j = ij_ref[1, pair]

    @pl.when(j == 0)
    def _():
        m_scr[...] = jnp.full(m_scr.shape, NEG_INF, F32)
        l_scr[...] = jnp.zeros(l_scr.shape, F32)
        acc_scr[...] = jnp.zeros(acc_scr.shape, F32)

    def step(causal, pad):
        mask = None
        if causal or pad:
            row = lax.broadcasted_iota(jnp.int32, (t, t), 0)
            col = lax.broadcasted_iota(jnp.int32, (t, t), 1)
            if causal:
                mask = col <= row
            if pad:
                mask = (col >= pad_front) if mask is None else mask & (col >= pad_front)
        lane = lax.broadcasted_iota(jnp.int32, (t, LANES), 1)
        scores, values = [], []
        for hd in range(DIFF_HEADS_PER_STEP):
            lanes = slice(hd * LANES, (hd + 1) * LANES)
            q, k, v = q_ref[:, lanes], k_ref[:, lanes], v_ref[:, lanes]
            values.append(jnp.concatenate([v, jnp.ones_like(v)], axis=1))
            for c in range(2):
                qc = jnp.where((lane >= c * DIFF_QK) & (lane < (c + 1) * DIFF_QK), q, jnp.zeros_like(q))
                s = lax.dot_general(qc, k, (((1,), (1,)), ((), ())), preferred_element_type=F32)
                if mask is not None:
                    s = jnp.where(mask, s, NEG_INF)
                scores.append(s)
        for n, s in enumerate(scores):
            m_prev = m_scr[n]
            m_new = jnp.maximum(m_prev, jnp.max(s, axis=-1, keepdims=True))
            alpha = jnp.exp2(m_prev - m_new)
            p = jnp.exp2(s - jnp.tile(m_new, (1, t // LANES)))
            pv = jnp.dot(p.astype(BF16), values[n // 2], preferred_element_type=F32)
            acc_scr[n] = alpha * acc_scr[n] + pv[:, :DIFF_V]
            l_scr[n] = alpha * l_scr[n] + pv[:, DIFF_V:]
            m_scr[n] = m_new

    _dispatch_by_mask_kind(i, j, pad_front, step)

    @pl.when(j == i)
    def _():
        lv = lam_ref[...]
        lam = (jnp.exp(jnp.sum(lv[0:1] * lv[1:2], axis=-1, keepdims=True))
               - jnp.exp(jnp.sum(lv[2:3] * lv[3:4], axis=-1, keepdims=True)) + lam_init)
        for hd in range(DIFF_HEADS_PER_STEP):
            a, b = 2 * hd, 2 * hd + 1
            o = acc_scr[a] / l_scr[a] - lam * (acc_scr[b] / l_scr[b])
            y = o * lax.rsqrt(jnp.mean(o * o, axis=-1, keepdims=True) + EPS) * gn_ref[...]
            o_ref[:, hd * LANES:(hd + 1) * LANES] = (y * (1.0 - lam_init)).astype(BF16)


def _diff_attention(proj, lam_vecs, gn, *, bsz, lp, pad_front, lam_init):
    t = SEQ_TILE
    nt = lp // t
    ij = _tile_pairs(nt, descending=False)
    hs = DIFF_HEADS_PER_STEP
    w = hs * LANES
    grid_spec = pltpu.PrefetchScalarGridSpec(
        num_scalar_prefetch=1,
        grid=(bsz, DIFF_HEADS // hs, ij.shape[1]),
        in_specs=[
            pl.BlockSpec((4, DIFF_QK), lambda b, h, p, ij: (0, 0)),
            pl.BlockSpec((t, w), lambda b, h, p, ij: (b * nt + ij[0, p], COL_DQ * GROUP_W // w + h)),
            pl.BlockSpec((t, w), lambda b, h, p, ij: (b * nt + ij[1, p], COL_DK * GROUP_W // w + h)),
            pl.BlockSpec((t, w), lambda b, h, p, ij: (b * nt + ij[1, p], COL_DV * GROUP_W // w + h)),
            pl.BlockSpec((1, DIFF_V), lambda b, h, p, ij: (0, 0)),
        ],
        out_specs=pl.BlockSpec((t, w), lambda b, h, p, ij: (b * nt + ij[0, p], h)),
        scratch_shapes=[pltpu.VMEM((2 * hs, t, LANES), F32), pltpu.VMEM((2 * hs, t, LANES), F32),
                        pltpu.VMEM((2 * hs, t, DIFF_V), F32)],
    )
    return pl.pallas_call(
        functools.partial(_diff_attn_kernel, t=t, pad_front=pad_front, lam_init=lam_init),
        grid_spec=grid_spec,
        out_shape=jax.ShapeDtypeStruct((bsz * lp, GROUP_W), BF16),
        compiler_params=_params("parallel", "parallel", "arbitrary"),
        name="diff_attn",
    )(ij, lam_vecs, proj, proj, proj, gn.reshape(1, DIFF_V))


def _sb_attn_kernel(ij_ref, q_ref, k_ref, v_ref, tri_ref, gn_ref, o_ref, run_scr, acc_scr,
                    *, t, sub, pad_front):
    pair = pl.program_id(2)
    i = ij_ref[0, pair]
    j = ij_ref[1, pair]

    @pl.when(j == i)
    def _():
        run_scr[...] = jnp.zeros(run_scr.shape, F32)
        acc_scr[...] = jnp.zeros(acc_scr.shape, F32)

    def block(qh, hh, lanes, r0, c0, mask):
        z = lax.dot_general(qh[r0:], k_ref[c0:c0 + sub, lanes], (((1,), (1,)), ((), ())),
                            preferred_element_type=F32)
        sign_bit = jnp.uint32(0x80000000)
        neg_abs = lax.bitcast_convert_type(lax.bitcast_convert_type(z, jnp.uint32) | sign_bit, F32)
        sp = jnp.maximum(z, 0.0) + jnp.log(1.0 + jnp.exp2(neg_abs)) * LOG2E
        if mask is not None:
            sp = jnp.where(mask, sp, 0.0)
        tail = jnp.dot(sp.astype(BF16), tri_ref[...], preferred_element_type=F32)
        run = run_scr[hh, r0:, :]
        log_w = z - tail - jnp.tile(run, (1, sub // LANES))
        if mask is not None:
            log_w = jnp.where(mask, log_w, NEG_INF)
        w = jnp.exp2(log_w)
        acc_scr[hh, r0:, :] += jnp.dot(w.astype(BF16), v_ref[c0:c0 + sub, lanes], preferred_element_type=F32)
        run_scr[hh, r0:, :] = run + jnp.sum(sp, axis=-1, keepdims=True)

    heads_per_block = LANES // SB_DIM

    def step(causal, pad):
        lane = lax.broadcasted_iota(jnp.int32, (t, LANES), 1)
        first_key = pad_front if pad else 0
        for blk in range(SB_BLOCKS_PER_STEP):
            lanes = slice(blk * LANES, (blk + 1) * LANES)
            q = q_ref[:, lanes]
            for hh in range(heads_per_block):
                qh = jnp.where((lane >= hh * SB_DIM) & (lane < (hh + 1) * SB_DIM), q, jnp.zeros_like(q))
                for c0 in reversed(range(0, t, sub)):
                    if c0 + sub <= first_key:
                        continue
                    r0 = c0 if causal else 0
                    mask = None
                    if causal or c0 < first_key:
                        row = r0 + lax.broadcasted_iota(jnp.int32, (t - r0, sub), 0)
                        col = c0 + lax.broadcasted_iota(jnp.int32, (t - r0, sub), 1)
                        if causal:
                            mask = col < row
                        if c0 < first_key:
                            mask = (col >= first_key) if mask is None else mask & (col >= first_key)
                    block(qh, blk * heads_per_block + hh, lanes, r0, c0, mask)

    _dispatch_by_mask_kind(i, j, pad_front, step)

    @pl.when(j == 0)
    def _():
        lane = lax.broadcasted_iota(jnp.int32, (t, LANES), 1)
        for blk in range(SB_BLOCKS_PER_STEP):
            y = jnp.zeros((t, LANES), F32)
            for hh in range(heads_per_block):
                in_head = (lane >= hh * SB_DIM) & (lane < (hh + 1) * SB_DIM)
                o = jnp.where(in_head, acc_scr[blk * heads_per_block + hh], 0.0)
                ms = jnp.sum(o * o, axis=-1, keepdims=True) * (1.0 / SB_DIM)
                y = y + o * lax.rsqrt(ms + EPS)
            o_ref[:, blk * LANES:(blk + 1) * LANES] = (y * gn_ref[...]).astype(BF16)


def _sb_attention(proj, tri, gn, *, bsz, lp, pad_front):
    t = SEQ_TILE
    nt = lp // t
    w = SB_BLOCKS_PER_STEP * LANES
    n_slot = SB_BLOCKS_PER_STEP * (LANES // SB_DIM)
    ij = _tile_pairs(nt, descending=True)
    grid_spec = pltpu.PrefetchScalarGridSpec(
        num_scalar_prefetch=1,
        grid=(bsz, GROUP_W // w, ij.shape[1]),
        in_specs=[
            pl.BlockSpec((t, w), lambda b, h, p, ij: (b * nt + ij[0, p], COL_SQ * GROUP_W // w + h)),
            pl.BlockSpec((t, w), lambda b, h, p, ij: (b * nt + ij[1, p], COL_SK * GROUP_W // w + h)),
            pl.BlockSpec((t, w), lambda b, h, p, ij: (b * nt + ij[1, p], COL_SV * GROUP_W // w + h)),
            pl.BlockSpec((MXU_DIM, MXU_DIM), lambda b, h, p, ij: (0, 0)),
            pl.BlockSpec((1, LANES), lambda b, h, p, ij: (0, 0)),
        ],
        out_specs=pl.BlockSpec((t, w), lambda b, h, p, ij: (b * nt + ij[0, p], h)),
        scratch_shapes=[pltpu.VMEM((n_slot, t, LANES), F32), pltpu.VMEM((n_slot, t, LANES), F32)],
    )
    return pl.pallas_call(
        functools.partial(_sb_attn_kernel, t=t, sub=MXU_DIM, pad_front=pad_front),
        grid_spec=grid_spec,
        out_shape=jax.ShapeDtypeStruct((bsz * lp, GROUP_W), BF16),
        compiler_params=_params("parallel", "parallel", "arbitrary"),
        name="sb_attn",
    )(ij, proj, proj, proj, tri, jnp.tile(gn.reshape(1, SB_DIM), (1, LANES // SB_DIM)))


def _s5_chunk_kernel(u_ref, d_ref, e_ref, y_ref, s_ref):
    _s5_intra(u_ref, d_ref, y_ref)
    _s5_state(u_ref, e_ref, s_ref)


def _s5_intra(u_ref, d_ref, y_ref):
    zero = jnp.zeros((LANES, LANES), BF16)
    for t0 in range(0, SSM_CHUNK, 2):
        w = jnp.concatenate(
            [jnp.concatenate([d_ref[0, t0 - s] if s <= t0 else zero, d_ref[0, t0 + 1 - s]], axis=1)
             for s in range(t0 + 2)], axis=0)
        y_ref[0, :, t0 * LANES:(t0 + 2) * LANES] = jnp.dot(
            u_ref[0, :, :(t0 + 2) * LANES], w, preferred_element_type=F32)


def _s5_state(u_ref, e_ref, o_ref):
    u = u_ref[0]
    cols = 4 * SSM_STATE
    row = lax.broadcasted_iota(jnp.int32, (LANES, cols), 0)
    for g in range(LANES // SSM_GROUP):
        in_group = (row >= g * SSM_GROUP) & (row < (g + 1) * SSM_GROUP)
        w = jnp.concatenate([jnp.where(in_group, e_ref[0, s], jnp.zeros((LANES, cols), BF16))
                             for s in range(SSM_CHUNK)], axis=0)
        o_ref[:, g * cols:(g + 1) * cols] = jnp.dot(u, w, preferred_element_type=F32)


def _s5_scan_kernel(s_ref, a1_ref, a2_ref, x_ref, *, n_chunk, bsz, groups):
    a1 = a1_ref[...]
    a2 = a2_ref[...]
    w = groups * LANES

    def body(c, carry):
        new = []
        for b in range(bsz):
            x, xs = carry[b]
            row = s_ref[pl.ds(b * n_chunk + c, 1), :]
            x_ref[pl.ds(b * n_chunk + c, 1), :] = x
            s_x = jnp.concatenate([row[:, g * 2 * LANES:g * 2 * LANES + LANES] for g in range(groups)], axis=1)
            s_xs = jnp.concatenate([row[:, g * 2 * LANES + LANES:(g + 1) * 2 * LANES] for g in range(groups)], axis=1)
            new.append((a1 * x + a2 * xs + s_x, a1 * xs - a2 * x + s_xs))
        return tuple(new)

    zero = jnp.zeros((1, w), F32)
    lax.fori_loop(0, n_chunk, body, tuple((zero, zero) for _ in range(bsz)))


def _s5_cross_kernel(yi_ref, x_ref, c_ref, y_ref):
    x = x_ref[...].astype(BF16)
    for t0 in range(0, SSM_CHUNK, 2):
        w = jnp.concatenate([c_ref[0, t0], c_ref[0, t0 + 1]], axis=1)
        cols = slice(t0 * LANES, (t0 + 2) * LANES)
        y_ref[0, :, cols] = yi_ref[0, :, cols] + jnp.dot(x, w, preferred_element_type=F32)


def _s5_out_kernel(y_ref, u_ref, d_ref, w_ref, b_ref, gn_ref, o_ref, y_scr):
    n_chunk = y_ref.shape[1]
    n_blk = GROUP_W // LANES
    for n in range(n_blk):
        for s in range(SSM_CHUNK):
            y_scr[n, pl.ds(s, n_chunk, stride=SSM_CHUNK), :] = y_ref[n, :, s * LANES:(s + 1) * LANES]
    y = jnp.concatenate([y_scr[n] for n in range(n_blk)], axis=1) + d_ref[...] * u_ref[...].astype(F32)
    g = 0.5 * y * (1.0 + jnp.tanh(math.sqrt(2.0 / math.pi) * (y + 0.044715 * (y * y * y))))
    gate = jnp.dot(g.astype(BF16), w_ref[...], preferred_element_type=F32) + b_ref[...]
    o = g * (1.0 / (1.0 + jnp.exp(-gate)))
    o_ref[...] = (o * lax.rsqrt(jnp.mean(o * o, axis=-1, keepdims=True) + EPS) * gn_ref[...]).astype(BF16)


def _s5_matrices(a_re, a_im, log_dt, b_re, b_im, c_re, c_im):
    hp = lax.Precision.HIGHEST
    t = SSM_CHUNK
    dt = jnp.exp(log_dt.astype(F32))[:, None]
    ar, ai = a_re.astype(F32), a_im.astype(F32)
    dpow = jnp.arange(t + 1, dtype=F32)[:, None, None]
    mag = jnp.exp(dpow * (ar * dt)[None])
    pr = mag * jnp.cos(dpow * (ai * dt)[None])
    pi = mag * jnp.sin(dpow * (ai * dt)[None])
    nr, ni = pr[1] - 1.0, pi[1]
    den = ar * ar + ai * ai
    fr = (nr * ar + ni * ai) / den
    fi = (ni * ar - nr * ai) / den
    br, bi = b_re.astype(F32), b_im.astype(F32)
    bbr = fr[..., None] * br - fi[..., None] * bi
    bbi = fr[..., None] * bi + fi[..., None] * br
    cr, ci = c_re.astype(F32), c_im.astype(F32)
    car = cr[None] * pr[:, :, None, :] - ci[None] * pi[:, :, None, :]
    cai = cr[None] * pi[:, :, None, :] + ci[None] * pr[:, :, None, :]
    kern = jnp.einsum('dghp,gpk->dghk', jnp.concatenate([car[:t], -cai[:t]], axis=-1),
                      jnp.concatenate([bbr, bbi], axis=1), precision=hp)
    gl = LANES // SSM_GROUP
    nb = SSM_NG // gl
    place = (jnp.arange(LANES)[None, None, :]
             == jnp.arange(gl)[:, None, None] * SSM_GROUP + jnp.arange(SSM_GROUP)[None, :, None]).astype(F32)
    kern_b = kern.reshape(t, nb, gl, SSM_GROUP, SSM_GROUP)
    d_blocks = jnp.einsum('dbghk,ghl->bdgkl', kern_b, place,
                          preferred_element_type=BF16).reshape(nb, t, LANES, LANES)
    rpow = (t - 1.0) - jnp.arange(t, dtype=F32)[:, None, None]
    rmag = jnp.exp(rpow * (ar * dt)[None])
    rev_r = (rmag * jnp.cos(rpow * (ai * dt)[None]))[:, :, None, :]
    rev_i = (rmag * jnp.sin(rpow * (ai * dt)[None]))[:, :, None, :]
    bbr_t, bbi_t = jnp.swapaxes(bbr, 1, 2)[None], jnp.swapaxes(bbi, 1, 2)[None]
    sre = rev_r * bbr_t - rev_i * bbi_t
    sim = rev_r * bbi_t + rev_i * bbr_t
    st = jnp.concatenate([sre, sim, sim, sre], axis=-1)
    e_blocks = jnp.transpose(st.astype(BF16).reshape(t, nb, LANES, 4 * SSM_STATE), (1, 0, 2, 3))
    cx = jnp.concatenate([car[1:], -cai[1:]], axis=-1)
    cx = cx.reshape(t, nb, gl, SSM_GROUP, 2 * SSM_STATE)
    c_blocks = jnp.einsum('tbghc,ghl->btgcl', cx, place,
                          preferred_element_type=BF16).reshape(nb, t, gl * 2 * SSM_STATE, LANES)
    a1 = jnp.concatenate([pr[t], pr[t]], axis=-1).reshape(1, SSM_NG * LANES)
    a2 = jnp.concatenate([-pi[t], pi[t]], axis=-1).reshape(1, SSM_NG * LANES)
    return d_blocks.astype(BF16), e_blocks.astype(BF16), c_blocks.astype(BF16), a1, a2


def _s5_mixer(proj, u_c, mats_all, layer, d, w_glu, b_glu, gn, *, bsz, lp):
    t = SSM_CHUNK
    nb = GROUP_W // LANES
    d_blocks, e_blocks, c_blocks = (m.reshape((-1,) + m.shape[2:]) for m in mats_all[:3])
    a1, a2 = mats_all[3][layer], mats_all[4][layer]
    first = layer * nb
    gl = LANES // SSM_GROUP
    cl = t * LANES
    sl = gl * 4 * SSM_STATE
    xl = gl * 2 * SSM_STATE
    n_chunk = lp // t
    nc = bsz * n_chunk
    rows = bsz * lp
    rt = nc // 2

    y_intra, s_end = pl.pallas_call(
        _s5_chunk_kernel,
        grid=(nb, nc // rt),
        in_specs=[
            pl.BlockSpec((1, rt, cl), lambda n, r: (n, r, 0)),
            pl.BlockSpec((1, t, LANES, LANES), lambda n, r: (first + n, 0, 0, 0)),
            pl.BlockSpec((1, t, LANES, 4 * SSM_STATE), lambda n, r: (first + n, 0, 0, 0)),
        ],
        out_specs=[pl.BlockSpec((1, rt, cl), lambda n, r: (n, r, 0)),
                   pl.BlockSpec((rt, sl), lambda n, r: (r, n))],
        out_shape=[jax.ShapeDtypeStruct((nb, nc, cl), F32),
                   jax.ShapeDtypeStruct((nc, nb * sl), F32)],
        compiler_params=_params("parallel", "parallel"),
        name="s5_chunk",
    )(u_c, d_blocks, e_blocks)

    x_start = pl.pallas_call(
        functools.partial(_s5_scan_kernel, n_chunk=n_chunk, bsz=bsz, groups=gl),
        grid=(nb,),
        in_specs=[
            pl.BlockSpec((nc, sl), lambda n: (0, n)),
            pl.BlockSpec((1, xl), lambda n: (0, n)),
            pl.BlockSpec((1, xl), lambda n: (0, n)),
        ],
        out_specs=pl.BlockSpec((nc, xl), lambda n: (0, n)),
        out_shape=jax.ShapeDtypeStruct((nc, nb * xl), F32),
        compiler_params=_params("parallel"),
        name="s5_scan",
    )(s_end, a1, a2)

    y_c = pl.pallas_call(
        _s5_cross_kernel,
        grid=(nb, nc // rt),
        in_specs=[
            pl.BlockSpec((1, rt, cl), lambda n, r: (n, r, 0)),
            pl.BlockSpec((rt, xl), lambda n, r: (r, n)),
            pl.BlockSpec((1, t, xl, LANES), lambda n, r: (first + n, 0, 0, 0)),
        ],
        out_specs=pl.BlockSpec((1, rt, cl), lambda n, r: (n, r, 0)),
        out_shape=jax.ShapeDtypeStruct((nb, nc, cl), F32),
        compiler_params=_params("parallel", "parallel"),
        name="s5_cross",
    )(y_intra, x_start, c_blocks)

    tm = SEQ_TILE
    vec = lambda: pl.BlockSpec((1, GROUP_W), lambda i: (0, 0))
    return pl.pallas_call(
        _s5_out_kernel,
        grid=(rows // tm,),
        in_specs=[
            pl.BlockSpec((nb, tm // t, cl), lambda i: (0, i, 0)),
            pl.BlockSpec((tm, GROUP_W), lambda i: (i, COL_SU)),
            vec(),
            pl.BlockSpec((GROUP_W, GROUP_W), lambda i: (0, 0)),
            vec(), vec(),
        ],
        out_specs=pl.BlockSpec((tm, GROUP_W), lambda i: (i, 0)),
        out_shape=jax.ShapeDtypeStruct((rows, GROUP_W), BF16),
        scratch_shapes=[pltpu.VMEM((nb, tm, LANES), F32)],
        compiler_params=_params("parallel"),
        name="s5_out",
    )(y_c, proj, d.reshape(1, GROUP_W), w_glu.astype(BF16), b_glu.reshape(1, GROUP_W), gn.reshape(1, GROUP_W))


def _retention_kernel(lg_ref, q_ref, k_ref, v_ref, g_ref, gn_ref, o_ref, st_scr, dmat_scr, *, t):
    c = pl.program_id(2)
    lg = lg_ref[0][:, :1]

    @pl.when(c == 0)
    def _():
        st_scr[...] = jnp.zeros(st_scr.shape, F32)
        ti = lax.broadcasted_iota(jnp.int32, (t, t), 0)
        si = lax.broadcasted_iota(jnp.int32, (t, t), 1)
        rel = (ti - si).astype(F32)
        dmat_scr[...] = jnp.where(ti >= si, jnp.exp(lg * jnp.maximum(rel, 0.0)), 0.0)

    q, k, v = q_ref[...], k_ref[...], v_ref[...]
    col = lax.broadcasted_iota(jnp.int32, (t, 1), 0).astype(F32)
    q_decay = jnp.exp(lg * (col + 1.0))
    k_decay = jnp.exp(lg * (t - 1.0 - col))
    chunk_decay = jnp.exp(lg * float(t))

    s = lax.dot_general(q, k, (((1,), (1,)), ((), ())), preferred_element_type=F32) * dmat_scr[...]
    state = st_scr[...]
    o = (jnp.dot(s.astype(BF16), v, preferred_element_type=F32)
         + jnp.dot(q, state.astype(BF16), preferred_element_type=F32) * q_decay)
    kd = (k.astype(F32) * k_decay).astype(BF16)
    st_scr[...] = chunk_decay * state + lax.dot_general(kd, v, (((0,), (0,)), ((), ())),
                                                        preferred_element_type=F32)

    mu = jnp.mean(o, axis=-1, keepdims=True)
    oc = o - mu
    y = oc * lax.rsqrt(jnp.mean(oc * oc, axis=-1, keepdims=True) + EPS) * gn_ref[...]
    gate = g_ref[...].astype(F32)
    o_ref[...] = (y * (gate / (1.0 + jnp.exp(-gate)))).astype(BF16)


def _retention(proj, gn, *, bsz, lp):
    t = SEQ_TILE
    nt = lp // t
    log_gamma = jnp.log1p(-jnp.exp2(-5.0 - jnp.arange(RET_HEADS, dtype=F32)))
    lg = jnp.broadcast_to(log_gamma[:, None, None], (RET_HEADS, 1, LANES))
    blk = lambda off: pl.BlockSpec((t, LANES), lambda b, h, c: (b * nt + c, off * 4 + h))
    return pl.pallas_call(
        functools.partial(_retention_kernel, t=t),
        grid=(bsz, RET_HEADS, nt),
        in_specs=[
            pl.BlockSpec((1, 1, LANES), lambda b, h, c: (h, 0, 0)),
            blk(COL_RQ), blk(COL_RK), blk(COL_RV), blk(COL_RG),
            pl.BlockSpec((1, RET_DIM), lambda b, h, c: (0, 0)),
        ],
        out_specs=pl.BlockSpec((t, LANES), lambda b, h, c: (b * nt + c, h)),
        out_shape=jax.ShapeDtypeStruct((bsz * lp, GROUP_W), BF16),
        scratch_shapes=[pltpu.VMEM((RET_DIM, RET_DIM), F32), pltpu.VMEM((t, t), F32)],
        compiler_params=_params("parallel", "parallel", "arbitrary"),
        name="retention",
    )(lg, proj, proj, proj, proj, gn.reshape(1, RET_DIM))


def _out_proj_kernel(a_ref, b_ref, c_ref, d_ref, w_ref, x_ref, g_ref, o_ref):
    acc = jnp.dot(a_ref[...], w_ref[0], preferred_element_type=F32)
    acc += jnp.dot(b_ref[...], w_ref[1], preferred_element_type=F32)
    acc += jnp.dot(c_ref[...], w_ref[2], preferred_element_type=F32)
    acc += jnp.dot(d_ref[...], w_ref[3], preferred_element_type=F32)
    y = acc * lax.rsqrt(jnp.mean(acc * acc, axis=-1, keepdims=True) + EPS) * g_ref[...]
    o_ref[...] = x_ref[...] + y


def _out_proj(mixes, w_all, layer, x2d, gain, *, rows_per_batch, drop_rows=0):
    rows = x2d.shape[0]
    tm = MXU_DIM
    assert drop_rows % tm == 0 and rows_per_batch % tm == 0
    per_batch, skip = rows_per_batch // tm, drop_rows // tm

    def out_idx(i):
        return ((i // per_batch) * (per_batch - skip) + jnp.maximum(i % per_batch - skip, 0), 0)

    mix_spec = pl.BlockSpec((tm, GROUP_W), lambda i: (i, 0))
    return pl.pallas_call(
        _out_proj_kernel,
        grid=(rows // tm,),
        in_specs=[
            mix_spec, mix_spec, mix_spec, mix_spec,
            pl.BlockSpec((4, GROUP_W, D_MODEL), lambda i: (layer, 0, 0)),
            pl.BlockSpec((tm, D_MODEL), lambda i: (i, 0)),
            pl.BlockSpec((1, D_MODEL), lambda i: (0, 0)),
        ],
        out_specs=pl.BlockSpec((tm, D_MODEL), out_idx),
        out_shape=jax.ShapeDtypeStruct((rows // per_batch * (per_batch - skip), D_MODEL), F32),
        compiler_params=_params("arbitrary"),
        name="out_proj",
    )(*mixes, w_all.reshape(-1, GROUP_W, D_MODEL), x2d, gain.reshape(1, D_MODEL))


def _ffn_kernel(x_ref, gpre_ref, wg_ref, wu_ref, wd_ref, gpost_ref, o_ref, h_scr):
    k = pl.program_id(1)

    @pl.when(k == 0)
    def _():
        x = x_ref[...]
        h_scr[...] = (x * lax.rsqrt(jnp.mean(x * x, axis=-1, keepdims=True) + EPS) * gpre_ref[...]).astype(BF16)

    h = h_scr[...]
    gate = jnp.dot(h, wg_ref[0], preferred_element_type=F32)
    up = jnp.dot(h, wu_ref[0], preferred_element_type=F32)
    act = (gate / (1.0 + jnp.exp(-gate)) * up).astype(BF16)
    part = jnp.dot(act, wd_ref[0], preferred_element_type=F32)

    @pl.when(k == 0)
    def _():
        o_ref[...] = part

    @pl.when(k > 0)
    def _():
        o_ref[...] += part

    @pl.when(k == pl.num_programs(1) - 1)
    def _():
        f = o_ref[...]
        o_ref[...] = x_ref[...] + f * lax.rsqrt(jnp.mean(f * f, axis=-1, keepdims=True) + EPS) * gpost_ref[...]


def _ffn(x2d, gpre, wg_all, wu_all, wd_all, layer, gpost):
    rows = x2d.shape[0]
    tm = next(m for m in (SEQ_TILE, 2 * MXU_DIM, MXU_DIM) if rows % m == 0)
    tf = GROUP_W
    vmem = (2 * 2 * tm * D_MODEL * 4 + tm * D_MODEL * 2 + 2 * 3 * D_MODEL * tf * 2
            + 2 * tm * tf * 4 + tm * tf * 2 + tm * D_MODEL * 4)
    return pl.pallas_call(
        _ffn_kernel,
        grid=(rows // tm, FFN_HIDDEN // tf),
        in_specs=[
            pl.BlockSpec((tm, D_MODEL), lambda i, k: (i, 0)),
            pl.BlockSpec((1, D_MODEL), lambda i, k: (0, 0)),
            pl.BlockSpec((1, D_MODEL, tf), lambda i, k: (layer, 0, k)),
            pl.BlockSpec((1, D_MODEL, tf), lambda i, k: (layer, 0, k)),
            pl.BlockSpec((1, tf, D_MODEL), lambda i, k: (layer, k, 0)),
            pl.BlockSpec((1, D_MODEL), lambda i, k: (0, 0)),
        ],
        out_specs=pl.BlockSpec((tm, D_MODEL), lambda i, k: (i, 0)),
        out_shape=jax.ShapeDtypeStruct((rows, D_MODEL), F32),
        scratch_shapes=[pltpu.VMEM((tm, D_MODEL), BF16)],
        compiler_params=pltpu.CompilerParams(dimension_semantics=("parallel", "arbitrary"),
                                             vmem_limit_bytes=vmem + VMEM_HEADROOM),
        name="ffn",
    )(x2d, gpre.reshape(1, D_MODEL), wg_all, wu_all, wd_all, gpost.reshape(1, D_MODEL))


def kernel(x, meta_tokens, norm_mix_pre, w_in, diff_lambda_q1, diff_lambda_k1, diff_lambda_q2, diff_lambda_k2, diff_norm, sb_norm, ssm_a_re, ssm_a_im, ssm_log_dt, ssm_b_re, ssm_b_im, ssm_c_re, ssm_c_im, ssm_d, ssm_w_glu, ssm_b_glu, ssm_norm, ret_norm, w_out, norm_mix_post, norm_ffn_pre, w_ffn_gate, w_ffn_up, w_ffn_down, norm_ffn_post):
    bsz, seq, d_model = x.shape
    assert d_model == D_MODEL and meta_tokens.shape == (N_META, D_MODEL)
    depth = w_in.shape[0]
    n_tok = N_META + seq
    lp = _padded_len(n_tok)
    pad_front = lp - n_tok
    first_real = pad_front + N_META

    meta = jnp.broadcast_to(meta_tokens[None].astype(x.dtype), (bsz, N_META, D_MODEL))
    h = jnp.concatenate([jnp.zeros((bsz, pad_front, D_MODEL), x.dtype), meta, x], axis=1)
    h = h.reshape(bsz * lp, D_MODEL)

    tabs = _rotary_tables(lp, pad_front)
    tri = (jnp.arange(MXU_DIM)[:, None] >= jnp.arange(MXU_DIM)[None, :]).astype(BF16)

    w_in_b, w_out_b = w_in.astype(BF16), w_out.astype(BF16)
    w_gate_b, w_up_b, w_down_b = w_ffn_gate.astype(BF16), w_ffn_up.astype(BF16), w_ffn_down.astype(BF16)

    s5_mats = jax.vmap(_s5_matrices)(ssm_a_re, ssm_a_im, ssm_log_dt, ssm_b_re, ssm_b_im, ssm_c_re, ssm_c_im)

    for l in range(depth):
        lam_init = 0.8 - 0.6 * math.exp(-0.3 * l)
        proj, su_chunks = _in_proj(h, norm_mix_pre[l], w_in_b, l, tabs, lp=lp, pad_front=pad_front)
        lam_vecs = jnp.stack([diff_lambda_q1[l], diff_lambda_k1[l], diff_lambda_q2[l], diff_lambda_k2[l]]).astype(F32)
        o_diff = _diff_attention(proj, lam_vecs, diff_norm[l], bsz=bsz, lp=lp, pad_front=pad_front, lam_init=lam_init)
        o_sb = _sb_attention(proj, tri, sb_norm[l], bsz=bsz, lp=lp, pad_front=pad_front)
        o_ssm = _s5_mixer(proj, su_chunks, s5_mats, l, ssm_d[l], ssm_w_glu[l], ssm_b_glu[l], ssm_norm[l], bsz=bsz, lp=lp)
        o_ret = _retention(proj, ret_norm[l], bsz=bsz, lp=lp)
        drop = first_real if (l == depth - 1 and first_real % MXU_DIM == 0) else 0
        h = _out_proj((o_diff, o_sb, o_ssm, o_ret), w_out_b, l, h, norm_mix_post[l],
                      rows_per_batch=lp, drop_rows=drop)
        h = _ffn(h, norm_ffn_pre[l], w_gate_b, w_up_b, w_down_b, l, norm_ffn_post[l])

    return h.reshape(bsz, -1, D_MODEL)[:, first_real - drop:]
```
